```python
import math
import jax, jax.numpy as jnp
from jax import lax
import numpy as np

D_MODEL = 2048
BATCH = 4
SEQ = 4096
DEPTH = 2

HEAD_DIM = 128
FOX_HEADS = D_MODEL // (2 * HEAD_DIM)
NSA_HEADS = D_MODEL // (2 * HEAD_DIM)
NSA_GROUP_SIZE = 4
NSA_KV_GROUPS = NSA_HEADS // NSA_GROUP_SIZE
FOX_W = FOX_HEADS * HEAD_DIM
NSA_W = NSA_HEADS * HEAD_DIM
NSA_KV_W = NSA_KV_GROUPS * HEAD_DIM
ROPE_DIM = HEAD_DIM // 4
ROPE_THETA = 500000.0
CMP_LEN = 32
CMP_STRIDE = 16
CMP_HIDDEN = 256
SLC_LEN = 64
SLC_TOP = 16
WINDOW = 512
FOX_Q_BLOCK = 128
NSA_Q_BLOCK = 64
FFN_HIDDEN = ((8 * D_MODEL + 3 * 256 - 1) // (3 * 256)) * 256
EPS = 1e-6
NEG_INF = -1e30
FORCE_SCORE = 1e6
IN_SIZES = [FOX_W, FOX_W, FOX_W, FOX_HEADS, NSA_W,
            NSA_KV_W, NSA_KV_W, NSA_KV_W, NSA_KV_W, NSA_KV_W, NSA_KV_W,
            NSA_HEADS * 3, D_MODEL, D_MODEL]
IN_W = sum(IN_SIZES)

kernel_name = "fox_nsa_parallel_hybrid"


def rms_norm(x, g):
    xf = x.astype(jnp.float32)
    y = xf * lax.rsqrt(jnp.mean(xf * xf, axis=-1, keepdims=True) + EPS) * g.astype(jnp.float32)
    return y.astype(x.dtype)


def partial_rope(x, pos):
    half = ROPE_DIM // 2
    inv_freq = jnp.power(ROPE_THETA, -jnp.arange(half, dtype=jnp.float32) * (2.0 / ROPE_DIM))
    ang = pos[:, None] * inv_freq[None, :]
    cos = jnp.cos(ang)[:, None, :]
    sin = jnp.sin(ang)[:, None, :]
    xf = x.astype(jnp.float32)
    x1 = xf[..., :half]
    x2 = xf[..., half:ROPE_DIM]
    out = jnp.concatenate([x1 * cos - x2 * sin, x2 * cos + x1 * sin, xf[..., ROPE_DIM:]], axis=-1)
    return out.astype(x.dtype)


def fox_attention(q, k, v, fg_logit, f_bias):
    B, T, H, Dh = q.shape
    QB = FOX_Q_BLOCK
    nblk = T // QB
    log_f = jax.nn.log_sigmoid(fg_logit.astype(jnp.float32) + f_bias.astype(jnp.float32))
    cum = jnp.cumsum(log_f, axis=1)
    cum_ht = cum.transpose(0, 2, 1)
    qb = q.reshape(B, nblk, QB, H, Dh).transpose(1, 0, 2, 3, 4)
    cb = cum.reshape(B, nblk, QB, H).transpose(1, 0, 3, 2)
    starts = jnp.arange(nblk) * QB
    kpos = jnp.arange(T)
    scale = HEAD_DIM ** -0.5

    def block(args):
        q_blk, c_blk, start = args
        s = jnp.einsum('bqhd,bkhd->bhqk', q_blk, k, preferred_element_type=jnp.float32) * scale
        s = s + (c_blk[..., :, None] - cum_ht[:, :, None, :])
        qpos = start + jnp.arange(QB)
        s = jnp.where(kpos[None, :] <= qpos[:, None], s, -jnp.inf)
        p = jax.nn.softmax(s, axis=-1)
        return jnp.einsum('bhqk,bkhd->bqhd', p.astype(v.dtype), v)

    o = lax.map(block, (qb, cb, starts))
    return o.transpose(1, 0, 2, 3, 4).reshape(B, T, H * Dh)


def compress_blocks(kv, pe, w1, w2):
    B, T, G, Dh = kv.shape
    nc = (T - CMP_LEN) // CMP_STRIDE + 1
    idx = np.arange(nc)[:, None] * CMP_STRIDE + np.arange(CMP_LEN)[None, :]
    blocks = kv[:, idx] + pe[None, None, :, None, :]
    blocks = blocks.transpose(0, 1, 3, 2, 4).reshape(B, nc, G, CMP_LEN * Dh)
    h = jax.nn.gelu(blocks @ w1)
    return h @ w2


def block_overlap(nc, nb):
    sc = np.arange(nc) * CMP_STRIDE
    ss = np.arange(nb) * SLC_LEN
    ov = np.minimum(sc[:, None] + CMP_LEN, ss[None, :] + SLC_LEN) - np.maximum(sc[:, None], ss[None, :])
    return (np.clip(ov, 0, None) / CMP_LEN).astype(np.float32)


def nsa_attention(q, kc, vc, ks, vs, kw, vw, gate_logit):
    B, T, H, Dh = q.shape
    G, HG = NSA_KV_GROUPS, NSA_GROUP_SIZE
    nc = kc.shape[1]
    nb = T // SLC_LEN
    n_sel = min(SLC_TOP, nb)
    QB = NSA_Q_BLOCK
    nq = T // QB
    scale = HEAD_DIM ** -0.5
    cmp_end = jnp.arange(nc) * CMP_STRIDE + CMP_LEN - 1
    overlap = jnp.asarray(block_overlap(nc, nb))
    ks_blk = ks.reshape(B, nb, SLC_LEN, G, Dh).transpose(0, 3, 1, 2, 4)
    vs_blk = vs.reshape(B, nb, SLC_LEN, G, Dh).transpose(0, 3, 1, 2, 4)
    kw_pad = jnp.pad(kw, ((0, 0), (WINDOW, 0), (0, 0), (0, 0)))
    vw_pad = jnp.pad(vw, ((0, 0), (WINDOW, 0), (0, 0), (0, 0)))
    gather = jax.vmap(jax.vmap(lambda blk, ix: blk[ix]))
    qc = q.reshape(B, nq, QB, G, HG, Dh).transpose(1, 0, 2, 3, 4, 5)
    gc = jax.nn.sigmoid(gate_logit.astype(jnp.float32)).reshape(B, nq, QB, G, HG, 3).transpose(1, 0, 2, 3, 4, 5)
    starts = jnp.arange(nq) * QB
    blk_ids = jnp.arange(nb)

    def block(args):
        q_blk, g_blk, start = args
        qpos = start + jnp.arange(QB)
        s_c = jnp.einsum('bqghd,bngd->bghqn', q_blk, kc, preferred_element_type=jnp.float32) * scale
        valid_c = cmp_end[None, :] <= qpos[:, None]
        s_c = jnp.where(valid_c, s_c, NEG_INF)
        p_c = jax.nn.softmax(s_c, axis=-1) * jnp.any(valid_c, axis=-1)[:, None].astype(jnp.float32)
        o_c = jnp.einsum('bghqn,bngd->bqghd', p_c.astype(vc.dtype), vc)
        imp = jnp.einsum('bghqn,nj->bgqj', p_c, overlap)
        cur = qpos // SLC_LEN
        forced = (blk_ids[None, :] == 0) | (blk_ids[None, :] == cur[:, None]) | (blk_ids[None, :] == cur[:, None] - 1)
        future = blk_ids[None, :] > cur[:, None]
        imp = jnp.where(forced, FORCE_SCORE, jnp.where(future, -1.0, imp))
        top_val, top_idx = lax.top_k(imp, n_sel)
        k_sel = gather(ks_blk, top_idx)
        v_sel = gather(vs_blk, top_idx)
        s_s = jnp.einsum('bqghd,bgqnld->bghqnl', q_blk, k_sel, preferred_element_type=jnp.float32) * scale
        kpos_s = top_idx[..., None] * SLC_LEN + jnp.arange(SLC_LEN)
        valid_s = (kpos_s <= qpos[None, None, :, None, None]) & (top_val >= 0.0)[..., None]
        s_s = jnp.where(valid_s[:, :, None], s_s, NEG_INF)
        p_s = jax.nn.softmax(s_s.reshape(B, G, HG, QB, n_sel * SLC_LEN), axis=-1)
        p_s = p_s.reshape(B, G, HG, QB, n_sel, SLC_LEN)
        o_s = jnp.einsum('bghqnl,bgqnld->bqghd', p_s.astype(v_sel.dtype), v_sel)
        k_win = lax.dynamic_slice_in_dim(kw_pad, start, WINDOW + QB, axis=1)
        v_win = lax.dynamic_slice_in_dim(vw_pad, start, WINDOW + QB, axis=1)
        kpos_w = start - WINDOW + jnp.arange(WINDOW + QB)
        dist = qpos[:, None] - kpos_w[None, :]
        valid_w = (dist >= 0) & (dist < WINDOW) & (kpos_w[None, :] >= 0)
        s_w = jnp.einsum('bqghd,bkgd->bghqk', q_blk, k_win, preferred_element_type=jnp.float32) * scale
        s_w = jnp.where(valid_w, s_w, NEG_INF)
        p_w = jax.nn.softmax(s_w, axis=-1)
        o_w = jnp.einsum('bghqk,bkgd->bqghd', p_w.astype(v_win.dtype), v_win)
        o = g_blk[..., 0:1] * o_c + g_blk[..., 1:2] * o_s + g_blk[..., 2:3] * o_w
        return o.astype(q.dtype)

    o = lax.map(block, (qc, gc, starts))
    return o.transpose(1, 0, 2, 3, 4, 5).reshape(B, T, H * Dh)


def hybrid_layer(x, n_mix_pre, n_mix_post, n_ffn_pre, n_ffn_post, w_in, f_bias,
                 ck_pe, ck_w1, ck_w2, cv_pe, cv_w1, cv_w2,
                 w_up_fox, w_up_nsa, w_out, w_gate, w_up, w_down):
    B, T, _ = x.shape
    a = rms_norm(x, n_mix_pre)
    proj = a @ w_in
    split_pts = np.cumsum(IN_SIZES)[:-1].tolist()
    (fq, fk, fv, ff, nq_, nkc, nvc, nks, nvs, nkw, nvw, ngate, g_fox, g_nsa) = jnp.split(proj, split_pts, axis=-1)
    pos = jnp.arange(T, dtype=jnp.float32)
    fq = fq.reshape(B, T, FOX_HEADS, HEAD_DIM)
    fk = fk.reshape(B, T, FOX_HEADS, HEAD_DIM)
    fv = fv.reshape(B, T, FOX_HEADS, HEAD_DIM)
    o_fox = fox_attention(fq, fk, fv, ff, f_bias)
    kv_shape = (B, T, NSA_KV_GROUPS, HEAD_DIM)
    nq_ = partial_rope(nq_.reshape(B, T, NSA_HEADS, HEAD_DIM), pos)
    nks = partial_rope(nks.reshape(kv_shape), pos)
    nkw = partial_rope(nkw.reshape(kv_shape), pos)
    kc = compress_blocks(nkc.reshape(kv_shape), ck_pe, ck_w1, ck_w2)
    vc = compress_blocks(nvc.reshape(kv_shape), cv_pe, cv_w1, cv_w2)
    nc = kc.shape[1]
    kc = partial_rope(kc, (jnp.arange(nc) * CMP_STRIDE + CMP_LEN - 1).astype(jnp.float32))
    o_nsa = nsa_attention(nq_, kc, vc, nks, nvs.reshape(kv_shape), nkw, nvw.reshape(kv_shape), ngate)
    mix = jax.nn.sigmoid(g_fox) * (o_fox @ w_up_fox) + jax.nn.sigmoid(g_nsa) * (o_nsa @ w_up_nsa)
    x = x + rms_norm(mix @ w_out, n_mix_post)
    a = rms_norm(x, n_ffn_pre)
    h = (jax.nn.silu(a @ w_gate) * (a @ w_up)) @ w_down
    return x + rms_norm(h, n_ffn_post)


def setup_inputs(seed: int = 0) -> dict:
    key = jax.random.key(seed)
    ks = jax.random.split(key, 24)
    f32 = jnp.float32

    def w(k, shape, fan_in):
        return jax.random.normal(k, shape, f32) * (fan_in ** -0.5)

    def gain(k):
        return 1.0 + 0.02 * jax.random.normal(k, (DEPTH, D_MODEL), f32)

    L = DEPTH
    return {
        "x": jax.random.normal(ks[0], (BATCH, SEQ, D_MODEL), f32),
        "norm_mix_pre": gain(ks[1]),
        "norm_mix_post": gain(ks[2]),
        "norm_ffn_pre": gain(ks[3]),
        "norm_ffn_post": gain(ks[4]),
        "w_in": w(ks[5], (L, D_MODEL, IN_W), D_MODEL),
        "fox_forget_bias": jax.random.uniform(ks[6], (L, FOX_HEADS), f32, 1.0, 4.0),
        "cmp_k_pe": 0.1 * jax.random.normal(ks[7], (L, CMP_LEN, HEAD_DIM), f32),
        "cmp_k_w1": w(ks[8], (L, CMP_LEN * HEAD_DIM, CMP_HIDDEN), CMP_LEN * HEAD_DIM),
        "cmp_k_w2": w(ks[9], (L, CMP_HIDDEN, HEAD_DIM), CMP_HIDDEN),
        "cmp_v_pe": 0.1 * jax.random.normal(ks[10], (L, CMP_LEN, HEAD_DIM), f32),
        "cmp_v_w1": w(ks[11], (L, CMP_LEN * HEAD_DIM, CMP_HIDDEN), CMP_LEN * HEAD_DIM),
        "cmp_v_w2": w(ks[12], (L, CMP_HIDDEN, HEAD_DIM), CMP_HIDDEN),
        "w_up_fox": w(ks[13], (L, FOX_W, D_MODEL), FOX_W),
        "w_up_nsa": w(ks[14], (L, NSA_W, D_MODEL), NSA_W),
        "w_out": w(ks[15], (L, D_MODEL, D_MODEL), D_MODEL),
        "w_ffn_gate": w(ks[16], (L, D_MODEL, FFN_HIDDEN), D_MODEL),
        "w_ffn_up": w(ks[17], (L, D_MODEL, FFN_HIDDEN), D_MODEL),
        "w_ffn_down": w(ks[18], (L, FFN_HIDDEN, D_MODEL), FFN_HIDDEN),
    }


def reference(x, norm_mix_pre, norm_mix_post, norm_ffn_pre, norm_ffn_post, w_in, fox_forget_bias,
              cmp_k_pe, cmp_k_w1, cmp_k_w2, cmp_v_pe, cmp_v_w1, cmp_v_w2,
              w_up_fox, w_up_nsa, w_out, w_ffn_gate, w_ffn_up, w_ffn_down):
    for l in range(DEPTH):
        x = hybrid_layer(x, norm_mix_pre[l], norm_mix_post[l], norm_ffn_pre[l], norm_ffn_post[l],
                         w_in[l], fox_forget_bias[l],
                         cmp_k_pe[l], cmp_k_w1[l], cmp_k_w2[l], cmp_v_pe[l], cmp_v_w1[l], cmp_v_w2[l],
                         w_up_fox[l], w_up_nsa[l], w_out[l], w_ffn_gate[l], w_ffn_up[l], w_ffn_down[l])
    return x
```

```python
import functools

import numpy as np
import jax
import jax.numpy as jnp
from jax import lax
from jax.experimental import pallas as pl
from jax.experimental.pallas import tpu as pltpu

D_MODEL = 2048
HEAD_DIM = 128
FOX_HEADS = 8
NSA_HEADS = 8
NSA_GROUP_SIZE = 4
NSA_KV_GROUPS = 2
FOX_W = FOX_HEADS * HEAD_DIM
NSA_W = NSA_HEADS * HEAD_DIM
NSA_KV_W = NSA_KV_GROUPS * HEAD_DIM
ROPE_DIM = HEAD_DIM // 4
ROPE_HALF = ROPE_DIM // 2
ROPE_THETA = 500000.0
CMP_LEN = 32
CMP_STRIDE = 16
CMP_HIDDEN = 256
SLC_LEN = 64
SLC_TOP = 16
WINDOW = 512
FFN_HIDDEN = 5632
EPS = 1e-6
NEG_INF = -1e30
FORCE_SCORE = 1e6
SCALE = HEAD_DIM ** -0.5
IN_SIZES = [FOX_W, FOX_W, FOX_W, FOX_HEADS, NSA_W,
            NSA_KV_W, NSA_KV_W, NSA_KV_W, NSA_KV_W, NSA_KV_W, NSA_KV_W,
            NSA_HEADS * 3, D_MODEL, D_MODEL]

LANES = 128
MXU_DT = jnp.bfloat16
VMEM_LIMIT = 56 * 1024 * 1024

CB_FQ, CB_FK, CB_FV, CB_NQ = 0, 8, 16, 24
CB_NKC, CB_NVC, CB_NKS, CB_NVS, CB_NKW, CB_NVW = 32, 34, 36, 38, 40, 42
CB_GF, CB_GN, CB_SM = 44, 60, 76
NP_COLS = 77 * LANES
SM_FF, SM_GATE = 0, FOX_HEADS

F32 = jnp.float32


def _cparams(n_axes):
    return pltpu.CompilerParams(dimension_semantics=("arbitrary",) * n_axes,
                                vmem_limit_bytes=VMEM_LIMIT)


def _rms(x):
    return x * lax.rsqrt(jnp.mean(x * x, axis=-1, keepdims=True) + EPS)


def _dot(a, b):
    return jnp.dot(a, b, preferred_element_type=F32)


def _dot_nt(a, b):
    return lax.dot_general(a, b, (((1,), (1,)), ((), ())), preferred_element_type=F32)


def _split3(x):
    hi = x.astype(MXU_DT)
    r1 = x - hi.astype(F32)
    mid = r1.astype(MXU_DT)
    lo = (r1 - mid.astype(F32)).astype(MXU_DT)
    return hi, mid, lo


def _rope(x, c, s):
    lane = lax.broadcasted_iota(jnp.int32, x.shape, 1)
    partner = jnp.where(lane < ROPE_HALF,
                        pltpu.roll(x, LANES - ROPE_HALF, 1),
                        pltpu.roll(x, ROPE_HALF, 1))
    return x * c + partner * s


def _sigmoid(z):
    return 1.0 / (1.0 + jnp.exp(-z))


def _inproj_body(x_ref, g_ref, w_ref, o_ref, a_ref):
    @pl.when(pl.program_id(1) == 0)
    def _():
        a_ref[...] = (_rms(x_ref[...]) * g_ref[...]).astype(a_ref.dtype)

    o_ref[...] = _dot(a_ref[...], w_ref[...])


def _inproj(x2, g, w):
    m, d = x2.shape
    n = w.shape[1]
    tm = min(512, m)
    tn = 11 * LANES
    return pl.pallas_call(
        _inproj_body,
        out_shape=jax.ShapeDtypeStruct((m, n), F32),
        grid=(m // tm, n // tn),
        in_specs=[pl.BlockSpec((tm, d), lambda i, j: (i, 0)),
                  pl.BlockSpec((1, d), lambda i, j: (0, 0)),
                  pl.BlockSpec((d, tn), lambda i, j: (0, j))],
        out_specs=pl.BlockSpec((tm, tn), lambda i, j: (i, j)),
        scratch_shapes=[pltpu.VMEM((tm, d), MXU_DT)],
        compiler_params=_cparams(2),
        name="inproj",
    )(x2, g, w)


def _foxcum_body(s_ref, b_ref, cum_ref, cumt_ref, *, T, CH):
    r = lax.broadcasted_iota(jnp.int32, (CH, CH), 0)
    c = lax.broadcasted_iota(jnp.int32, (CH, CH), 1)
    tri = jnp.where(c <= r, 1.0, 0.0).astype(MXU_DT)
    carry = jnp.zeros((1, LANES), F32)
    for ci in range(T // CH):
        z = s_ref[ci * CH:(ci + 1) * CH, :] + b_ref[...]
        lf = jnp.minimum(z, 0.0) - jnp.log1p(jnp.exp(-jnp.abs(z)))
        hi, mid, lo = _split3(lf)
        out = (_dot(tri, hi) + _dot(tri, mid)) + _dot(tri, lo) + carry
        cum_ref[ci * CH:(ci + 1) * CH, :] = out
        cumt_ref[:, ci * CH:(ci + 1) * CH] = out.T
        carry = out[CH - 1:CH, :]


def _foxcum(proj, bias_row, B, T):
    CH = 256
    return pl.pallas_call(
        functools.partial(_foxcum_body, T=T, CH=CH),
        out_shape=(jax.ShapeDtypeStruct((B * T, LANES), F32),
                   jax.ShapeDtypeStruct((B * LANES, T), F32)),
        grid=(B,),
        in_specs=[pl.BlockSpec((T, LANES), lambda b: (b, CB_SM)),
                  pl.BlockSpec((1, LANES), lambda b: (0, 0))],
        out_specs=(pl.BlockSpec((T, LANES), lambda b: (b, 0)),
                   pl.BlockSpec((LANES, T), lambda b: (b, 0))),
        compiler_params=_cparams(1),
        name="foxcum",
    )(proj, bias_row)


def _online_softmax_step(s, vt, m_ref, l_ref, acc_ref):
    m_old = m_ref[...]
    m_new = jnp.maximum(m_old, jnp.max(s, axis=1, keepdims=True))
    alpha = jnp.exp(m_old - m_new)
    p = jnp.exp(s - m_new)
    l_ref[...] = alpha * l_ref[...] + jnp.sum(p, axis=1, keepdims=True)
    acc_ref[...] = alpha * acc_ref[...] + _dot(p.astype(MXU_DT), vt)
    m_ref[...] = m_new


def _fox_body(q_ref, k_ref, v_ref, cq_ref, ck_ref, o_ref,
              kb_ref, vb_ref, m_ref, l_ref, acc_ref, *, TQ, TK):
    h = pl.program_id(1)
    i = pl.program_id(2)

    @pl.when(i == 0)
    def _():
        kb_ref[...] = k_ref[...].astype(MXU_DT)
        vb_ref[...] = v_ref[...].astype(MXU_DT)

    qb = q_ref[...].astype(MXU_DT)
    lane = lax.broadcasted_iota(jnp.int32, (TQ, LANES), 1)
    cq = jnp.sum(jnp.where(lane == h, cq_ref[...], 0.0), axis=1, keepdims=True)
    qpos = i * TQ + lax.broadcasted_iota(jnp.int32, (TQ, TK), 0)
    kofs = lax.broadcasted_iota(jnp.int32, (TQ, TK), 1)
    m_ref[...] = jnp.full(m_ref.shape, NEG_INF, F32)
    l_ref[...] = jnp.zeros(l_ref.shape, F32)
    acc_ref[...] = jnp.zeros(acc_ref.shape, F32)

    def step(j, carry):
        k0 = pl.multiple_of(j * TK, TK)
        kt = kb_ref[pl.ds(k0, TK), :]
        vt = vb_ref[pl.ds(k0, TK), :]
        ck = ck_ref[pl.ds(h, 1), pl.ds(k0, TK)]
        s = _dot_nt(qb, kt) * SCALE + (cq - ck)
        s = jnp.where(k0 + kofs <= qpos, s, NEG_INF)
        _online_softmax_step(s, vt, m_ref, l_ref, acc_ref)
        return carry

    nk = ((i + 1) * TQ + TK - 1) // TK
    lax.fori_loop(0, nk, step, 0)
    o_ref[...] = (acc_ref[...] / l_ref[...]).astype(o_ref.dtype)


def _fox(proj, cum, cumt, B, T):
    TQ = TK = 256
    nq = T // TQ
    return pl.pallas_call(
        functools.partial(_fox_body, TQ=TQ, TK=TK),
        out_shape=jax.ShapeDtypeStruct((B * T, FOX_W), MXU_DT),
        grid=(B, FOX_HEADS, nq),
        in_specs=[pl.BlockSpec((TQ, LANES), lambda b, h, i: (b * nq + i, CB_FQ + h)),
                  pl.BlockSpec((T, LANES), lambda b, h, i: (b, CB_FK + h)),
                  pl.BlockSpec((T, LANES), lambda b, h, i: (b, CB_FV + h)),
                  pl.BlockSpec((TQ, LANES), lambda b, h, i: (b * nq + i, 0)),
                  pl.BlockSpec((8, T), lambda b, h, i: (b * (LANES // 8), 0))],
        out_specs=pl.BlockSpec((TQ, LANES), lambda b, h, i: (b * nq + i, h)),
        scratch_shapes=[pltpu.VMEM((T, LANES), MXU_DT),
                        pltpu.VMEM((T, LANES), MXU_DT),
                        pltpu.VMEM((TQ, 1), F32),
                        pltpu.VMEM((TQ, 1), F32),
                        pltpu.VMEM((TQ, LANES), F32)],
        compiler_params=_cparams(3),
        name="fox",
    )(proj, proj, proj, cum, cumt)


def _gelu_tanh(x):
    return 0.5 * x * (1.0 + jnp.tanh(np.sqrt(2.0 / np.pi).astype(np.float32)
                                     * (x + 0.044715 * (x * x * x))))


def _compress_one(x_ref, pe_ref, w1_ref, w2_ref, NCP):
    h0 = jnp.zeros((NCP, CMP_HIDDEN), F32)
    h1 = jnp.zeros((NCP, CMP_HIDDEN), F32)
    for l in range(CMP_STRIDE):
        xl = x_ref[pl.ds(l, NCP, stride=CMP_STRIDE), :]
        a0 = (xl + pe_ref[l:l + 1, :]).astype(MXU_DT)
        a1 = (xl + pe_ref[CMP_STRIDE + l:CMP_STRIDE + l + 1, :]).astype(MXU_DT)
        h0 = h0 + _dot(a0, w1_ref[l * HEAD_DIM:(l + 1) * HEAD_DIM, :])
        h1 = h1 + _dot(a1, w1_ref[(CMP_STRIDE + l) * HEAD_DIM:(CMP_STRIDE + l + 1) * HEAD_DIM, :])
    hsum = h0 + pltpu.roll(h1, NCP - 1, 0)
    return _dot(_gelu_tanh(hsum).astype(MXU_DT), w2_ref[...])


def _compress_body(xk_ref, xv_ref, pek_ref, w1k_ref, w2k_ref, pev_ref, w1v_ref, w2v_ref,
                   cc_ref, sc_ref, kc_ref, vc_ref, *, NCP):
    kc = _compress_one(xk_ref, pek_ref, w1k_ref, w2k_ref, NCP)
    kc_ref[0, 0] = _rope(kc, cc_ref[...], sc_ref[...]).astype(kc_ref.dtype)
    vc = _compress_one(xv_ref, pev_ref, w1v_ref, w2v_ref, NCP)
    vc_ref[0, 0] = vc.astype(vc_ref.dtype)


def _compress(proj, pek, w1k, w2k, pev, w1v, w2v, cc, sc, B, T):
    NCP = T // CMP_STRIDE
    full = lambda shape: pl.BlockSpec(shape, lambda b, g: (0,) * len(shape))
    out_sds = jax.ShapeDtypeStruct((B, NSA_KV_GROUPS, NCP, HEAD_DIM), MXU_DT)
    out_spec = pl.BlockSpec((1, 1, NCP, HEAD_DIM), lambda b, g: (b, g, 0, 0))
    return pl.pallas_call(
        functools.partial(_compress_body, NCP=NCP),
        out_shape=(out_sds, out_sds),
        grid=(B, NSA_KV_GROUPS),
        in_specs=[pl.BlockSpec((T, LANES), lambda b, g: (b, CB_NKC + g)),
                  pl.BlockSpec((T, LANES), lambda b, g: (b, CB_NVC + g)),
                  full((CMP_LEN, HEAD_DIM)), full((CMP_LEN * HEAD_DIM, CMP_HIDDEN)),
                  full((CMP_HIDDEN, HEAD_DIM)),
                  full((CMP_LEN, HEAD_DIM)), full((CMP_LEN * HEAD_DIM, CMP_HIDDEN)),
                  full((CMP_HIDDEN, HEAD_DIM)),
                  full((NCP, LANES)), full((NCP, LANES))],
        out_specs=(out_spec, out_spec),
        compiler_params=_cparams(2),
        name="compress",
    )(proj, proj, pek, w1k, w2k, pev, w1v, w2v, cc, sc)


def _nsa_body(q_ref, sm_ref, cq_ref, sq_ref, kc_ref, vc_ref, ks_ref, vs_ref, kw_ref, vw_ref,
              ct_ref, st_ref, ovt_ref, o_ref,
              ksb_ref, vsb_ref, kwb_ref, vwb_ref, m_ref, l_ref, acc_ref,
              *, T, TQ, TK, WSP, NB, NCP, NSEL):
    g = pl.program_id(1)
    i = pl.program_id(2)
    HG = NSA_GROUP_SIZE
    R = HG * TQ
    RC = 512

    @pl.when(i == 0)
    def _():
        for c0 in range(0, T, RC):
            sl = slice(c0, c0 + RC)
            ct = ct_ref[sl, :]
            st = st_ref[sl, :]
            ksb_ref[sl, :] = _rope(ks_ref[sl, :], ct, st).astype(MXU_DT)
            kwb_ref[sl, :] = _rope(kw_ref[sl, :], ct, st).astype(MXU_DT)
            vsb_ref[sl, :] = vs_ref[sl, :].astype(MXU_DT)
            vwb_ref[sl, :] = vw_ref[sl, :].astype(MXU_DT)

    q0 = i * TQ
    cq = cq_ref[...]
    sq = sq_ref[...]
    qb = jnp.concatenate(
        [_rope(q_ref[:, hh * HEAD_DIM:(hh + 1) * HEAD_DIM], cq, sq) for hh in range(HG)],
        axis=0).astype(MXU_DT)
    rrow = lax.broadcasted_iota(jnp.int32, (R, 1), 0)
    qpos = q0 + (rrow & (TQ - 1))

    sc = _dot_nt(qb, kc_ref[0, 0]) * SCALE
    ncol = lax.broadcasted_iota(jnp.int32, (R, NCP), 1)
    sc = jnp.where(ncol * CMP_STRIDE + (CMP_LEN - 1) <= qpos, sc, NEG_INF)
    pc = jnp.exp(sc - jnp.max(sc, axis=1, keepdims=True))
    pc = pc / jnp.sum(pc, axis=1, keepdims=True)
    pc = pc * jnp.where(qpos >= CMP_LEN - 1, 1.0, 0.0)
    o_c = _dot(pc.astype(MXU_DT), vc_ref[0, 0])

    psum = (pc[0:TQ] + pc[TQ:2 * TQ]) + (pc[2 * TQ:3 * TQ] + pc[3 * TQ:4 * TQ])
    hi, mid, lo = _split3(psum)
    ovt = ovt_ref[...]
    imp = (_dot_nt(ovt, hi) + _dot_nt(ovt, mid)) + _dot_nt(ovt, lo)
    jblk = lax.broadcasted_iota(jnp.int32, (NB, TQ), 0)
    cur = (q0 + lax.broadcasted_iota(jnp.int32, (NB, TQ), 1)) >> 6
    forced = (jblk == 0) | (jblk == cur) | (jblk == cur - 1)
    imp = jnp.where(forced, FORCE_SCORE, jnp.where(jblk > cur, -1.0, imp))
    rank = jnp.zeros((NB, TQ), F32)
    for jp in range(NB):
        row = imp[jp:jp + 1, :]
        ge = jnp.where(row >= imp, 1.0, 0.0)
        gt = jnp.where(row > imp, 1.0, 0.0)
        rank = rank + jnp.where(jblk > jp, ge, gt)
    sel_t = jnp.where((rank < NSEL) & (imp >= 0.0), 1.0, 0.0)
    if NB < LANES:
        sel_t = jnp.concatenate([sel_t, jnp.zeros((LANES - NB, TQ), F32)], axis=0)
    sel = sel_t.T.astype(MXU_DT)

    m_ref[...] = jnp.full(m_ref.shape, NEG_INF, F32)
    l_ref[...] = jnp.zeros(l_ref.shape, F32)
    acc_ref[...] = jnp.zeros(acc_ref.shape, F32)
    eblk = lax.broadcasted_iota(jnp.int32, (LANES, TK), 0)
    ekey = lax.broadcasted_iota(jnp.int32, (LANES, TK), 1)
    kofs = lax.broadcasted_iota(jnp.int32, (R, TK), 1)

    def step(j, carry):
        k0 = pl.multiple_of(j * TK, TK)
        kt = ksb_ref[pl.ds(k0, TK), :]
        vt = vsb_ref[pl.ds(k0, TK), :]
        s = _dot_nt(qb, kt) * SCALE
        expand = jnp.where(((k0 + ekey) >> 6) == eblk, 1.0, 0.0).astype(MXU_DT)
        mk = _dot(sel, expand)
        mk = jnp.concatenate([mk] * HG, axis=0)
        s = jnp.where((mk > 0.5) & (k0 + kofs <= qpos), s, NEG_INF)
        _online_softmax_step(s, vt, m_ref, l_ref, acc_ref)
        return carry

    nk = ((i + 1) * TQ + TK - 1) // TK
    lax.fori_loop(0, nk, step, 0)
    o_s = acc_ref[...] / l_ref[...]

    w0 = pl.multiple_of(jnp.clip((i + 1) * TQ - WSP, 0, T - WSP), 64)
    kt = kwb_ref[pl.ds(w0, WSP), :]
    vt = vwb_ref[pl.ds(w0, WSP), :]
    sw = _dot_nt(qb, kt) * SCALE
    dist = qpos - (w0 + lax.broadcasted_iota(jnp.int32, (R, WSP), 1))
    sw = jnp.where((dist >= 0) & (dist < WINDOW), sw, NEG_INF)
    pw = jnp.exp(sw - jnp.max(sw, axis=1, keepdims=True))
    pw = pw / jnp.sum(pw, axis=1, keepdims=True)
    o_w = _dot(pw.astype(MXU_DT), vt)

    sm = sm_ref[...]
    lane = lax.broadcasted_iota(jnp.int32, (TQ, LANES), 1)
    for hh in range(HG):
        col = SM_GATE + 3 * (g * HG + hh)
        gate = [_sigmoid(jnp.sum(jnp.where(lane == col + c, sm, 0.0), axis=1, keepdims=True))
                for c in range(3)]
        rows = slice(hh * TQ, (hh + 1) * TQ)
        o = gate[0] * o_c[rows] + gate[1] * o_s[rows] + gate[2] * o_w[rows]
        o_ref[:, hh * HEAD_DIM:(hh + 1) * HEAD_DIM] = o.astype(o_ref.dtype)


def _nsa(proj, kc, vc, ct, st, ovt, B, T):
    TQ = 128
    TK = 256
    WSP = WINDOW + TQ
    NB = T // SLC_LEN
    NCP = T // CMP_STRIDE
    nq = T // TQ
    HG = NSA_GROUP_SIZE
    qw = HG * HEAD_DIM
    kv_spec = lambda cb: pl.BlockSpec((T, LANES), lambda b, g, i: (b, cb + g))
    cmp_spec = pl.BlockSpec((1, 1, NCP, HEAD_DIM), lambda b, g, i: (b, g, 0, 0))
    return pl.pallas_call(
        functools.partial(_nsa_body, T=T, TQ=TQ, TK=TK, WSP=WSP, NB=NB, NCP=NCP,
                          NSEL=min(SLC_TOP, NB)),
        out_shape=jax.ShapeDtypeStruct((B * T, NSA_W), MXU_DT),
        grid=(B, NSA_KV_GROUPS, nq),
        in_specs=[pl.BlockSpec((TQ, qw), lambda b, g, i: (b * nq + i, CB_NQ * LANES // qw + g)),
                  pl.BlockSpec((TQ, LANES), lambda b, g, i: (b * nq + i, CB_SM)),
                  pl.BlockSpec((TQ, LANES), lambda b, g, i: (i, 0)),
                  pl.BlockSpec((TQ, LANES), lambda b, g, i: (i, 0)),
                  cmp_spec, cmp_spec,
                  kv_spec(CB_NKS), kv_spec(CB_NVS), kv_spec(CB_NKW), kv_spec(CB_NVW),
                  pl.BlockSpec((T, LANES), lambda b, g, i: (0, 0)),
                  pl.BlockSpec((T, LANES), lambda b, g, i: (0, 0)),
                  pl.BlockSpec((NB, NCP), lambda b, g, i: (0, 0))],
        out_specs=pl.BlockSpec((TQ, qw), lambda b, g, i: (b * nq + i, g)),
        scratch_shapes=[pltpu.VMEM((T, LANES), MXU_DT)] * 4
                       + [pltpu.VMEM((HG * TQ, 1), F32),
                          pltpu.VMEM((HG * TQ, 1), F32),
                          pltpu.VMEM((HG * TQ, LANES), F32)],
        compiler_params=_cparams(3),
        name="nsa",
    )(proj, proj, ct, st, kc, vc, proj, proj, proj, proj, ct, st, ovt)


def _merge_body(of_ref, on_ref, wf_ref, wn_ref, gf_ref, gn_ref, o_ref):
    yf = _dot(of_ref[...], wf_ref[...])
    yn = _dot(on_ref[...], wn_ref[...])
    o_ref[...] = (_sigmoid(gf_ref[...]) * yf + _sigmoid(gn_ref[...]) * yn).astype(o_ref.dtype)


def _merge(o_fox, o_nsa, wf, wn, proj):
    m = o_fox.shape[0]
    tm = min(512, m)
    tn = 512
    gfb = CB_GF * LANES // tn
    gnb = CB_GN * LANES // tn
    return pl.pallas_call(
        _merge_body,
        out_shape=jax.ShapeDtypeStruct((m, D_MODEL), MXU_DT),
        grid=(m // tm, D_MODEL // tn),
        in_specs=[pl.BlockSpec((tm, FOX_W), lambda i, j: (i, 0)),
                  pl.BlockSpec((tm, NSA_W), lambda i, j: (i, 0)),
                  pl.BlockSpec((FOX_W, tn), lambda i, j: (0, j)),
                  pl.BlockSpec((NSA_W, tn), lambda i, j: (0, j)),
                  pl.BlockSpec((tm, tn), lambda i, j: (i, gfb + j)),
                  pl.BlockSpec((tm, tn), lambda i, j: (i, gnb + j))],
        out_specs=pl.BlockSpec((tm, tn), lambda i, j: (i, j)),
        compiler_params=_cparams(2),
        name="merge",
    )(o_fox, o_nsa, wf, wn, proj, proj)


def _outproj_body(a_ref, w_ref, g_ref, x_ref, o_ref):
    y = _dot(a_ref[...], w_ref[...])
    o_ref[...] = x_ref[...] + _rms(y) * g_ref[...]


def _outproj(mix, w, g, x2):
    m, d = x2.shape
    tm = min(512, m)
    return pl.pallas_call(
        _outproj_body,
        out_shape=jax.ShapeDtypeStruct((m, d), F32),
        grid=(m // tm,),
        in_specs=[pl.BlockSpec((tm, d), lambda i: (i, 0)),
                  pl.BlockSpec((d, d), lambda i: (0, 0)),
                  pl.BlockSpec((1, d), lambda i: (0, 0)),
                  pl.BlockSpec((tm, d), lambda i: (i, 0))],
        out_specs=pl.BlockSpec((tm, d), lambda i: (i, 0)),
        compiler_params=_cparams(1),
        name="outproj",
    )(mix, w, g, x2)


def _ffn_up_body(x_ref, g_ref, wg_ref, wu_ref, o_ref, a_ref):
    @pl.when(pl.program_id(1) == 0)
    def _():
        a_ref[...] = (_rms(x_ref[...]) * g_ref[...]).astype(a_ref.dtype)

    a = a_ref[...]
    hg = _dot(a, wg_ref[...])
    hu = _dot(a, wu_ref[...])
    o_ref[...] = (hg * _sigmoid(hg) * hu).astype(o_ref.dtype)


def _ffn_up(x2, g, wg, wu):
    m, d = x2.shape
    f = wg.shape[1]
    tm = min(1024, m)
    tn = 512
    return pl.pallas_call(
        _ffn_up_body,
        out_shape=jax.ShapeDtypeStruct((m, f), MXU_DT),
        grid=(m // tm, f // tn),
        in_specs=[pl.BlockSpec((tm, d), lambda i, j: (i, 0)),
                  pl.BlockSpec((1, d), lambda i, j: (0, 0)),
                  pl.BlockSpec((d, tn), lambda i, j: (0, j)),
                  pl.BlockSpec((d, tn), lambda i, j: (0, j))],
        out_specs=pl.BlockSpec((tm, tn), lambda i, j: (i, j)),
        scratch_shapes=[pltpu.VMEM((tm, d), MXU_DT)],
        compiler_params=_cparams(2),
        name="ffn_up",
    )(x2, g, wg, wu)


def _ffn_down_body(h_ref, w_ref, g_ref, x_ref, o_ref, acc_ref):
    k = pl.program_id(1)

    @pl.when(k == 0)
    def _():
        acc_ref[...] = jnp.zeros(acc_ref.shape, F32)

    acc_ref[...] += _dot(h_ref[...], w_ref[...])

    @pl.when(k == pl.num_programs(1) - 1)
    def _():
        o_ref[...] = x_ref[...] + _rms(acc_ref[...]) * g_ref[...]


def _ffn_down(h, w, g, x2):
    m, d = x2.shape
    f = h.shape[1]
    tm = min(512, m)
    tk = 11 * LANES
    return pl.pallas_call(
        _ffn_down_body,
        out_shape=jax.ShapeDtypeStruct((m, d), F32),
        grid=(m // tm, f // tk),
        in_specs=[pl.BlockSpec((tm, tk), lambda i, k: (i, k)),
                  pl.BlockSpec((tk, d), lambda i, k: (k, 0)),
                  pl.BlockSpec((1, d), lambda i, k: (0, 0)),
                  pl.BlockSpec((tm, d), lambda i, k: (i, 0))],
        out_specs=pl.BlockSpec((tm, d), lambda i, k: (i, 0)),
        scratch_shapes=[pltpu.VMEM((tm, d), F32)],
        compiler_params=_cparams(2),
        name="ffn_down",
    )(h, w, g, x2)


def _pack_w_in(w):
    offs = np.cumsum([0] + IN_SIZES)
    seg = lambda k: w[:, offs[k]:offs[k + 1]]
    small = jnp.concatenate(
        [seg(3), seg(11), jnp.zeros((w.shape[0], LANES - FOX_HEADS - 3 * NSA_HEADS), w.dtype)], axis=1)
    cols = [seg(k) for k in (0, 1, 2, 4, 5, 6, 7, 8, 9, 10, 12, 13)] + [small]
    return jnp.concatenate(cols, axis=1).astype(MXU_DT)


def _rope_tables(pos):
    inv_freq = jnp.power(ROPE_THETA, -jnp.arange(ROPE_HALF, dtype=F32) * (2.0 / ROPE_DIM))
    ang = pos[:, None] * inv_freq[None, :]
    cos, sin = jnp.cos(ang), jnp.sin(ang)
    n = pos.shape[0]
    pad = HEAD_DIM - ROPE_DIM
    c = jnp.concatenate([cos, cos, jnp.ones((n, pad), F32)], axis=1)
    s = jnp.concatenate([-sin, sin, jnp.zeros((n, pad), F32)], axis=1)
    return c, s


def _overlap_t(T):
    nb, ncp = T // SLC_LEN, T // CMP_STRIDE
    sc = np.arange(ncp) * CMP_STRIDE
    ss = np.arange(nb) * SLC_LEN
    ov = np.minimum(sc[None, :] + CMP_LEN, ss[:, None] + SLC_LEN) - np.maximum(sc[None, :], ss[:, None])
    ov = np.clip(ov, 0, None) / CMP_LEN
    ov[:, ncp - 1] = 0.0
    return jnp.asarray(ov, dtype=MXU_DT)


def _layer(x2, B, T, n_mix_pre, n_mix_post, n_ffn_pre, n_ffn_post, w_in, f_bias,
           ck_pe, ck_w1, ck_w2, cv_pe, cv_w1, cv_w2, w_up_fox, w_up_nsa, w_out, w_gate, w_up, w_down,
           tables):
    ct, st, cc, sc, ovt = tables
    row = lambda v: v.reshape(1, -1).astype(F32)
    bf = lambda w: w.astype(MXU_DT)
    proj = _inproj(x2, row(n_mix_pre), _pack_w_in(w_in))
    bias_row = jnp.pad(f_bias.astype(F32), (0, LANES - FOX_HEADS)).reshape(1, LANES)
    cum, cumt = _foxcum(proj, bias_row, B, T)
    o_fox = _fox(proj, cum, cumt, B, T)
    kc, vc = _compress(proj, ck_pe, bf(ck_w1), bf(ck_w2), cv_pe, bf(cv_w1), bf(cv_w2), cc, sc, B, T)
    o_nsa = _nsa(proj, kc, vc, ct, st, ovt, B, T)
    mix = _merge(o_fox, o_nsa, bf(w_up_fox), bf(w_up_nsa), proj)
    x2 = _outproj(mix, bf(w_out), row(n_mix_post), x2)
    h = _ffn_up(x2, row(n_ffn_pre), bf(w_gate), bf(w_up))
    return _ffn_down(h, bf(w_down), row(n_ffn_post), x2)


@jax.jit
def kernel(x, norm_mix_pre, norm_mix_post, norm_ffn_pre, norm_ffn_post, w_in, fox_forget_bias, cmp_k_pe, cmp_k_w1, cmp_k_w2, cmp_v_pe, cmp_v_w1, cmp_v_w2, w_up_fox, w_up_nsa, w_out, w_ffn_gate, w_ffn_up, w_ffn_down):
    B, T, D = x.shape
    ct, st = _rope_tables(jnp.arange(T, dtype=F32))
    ncp = T // CMP_STRIDE
    cc, sc = _rope_tables((jnp.arange(ncp) * CMP_STRIDE + CMP_LEN - 1).astype(F32))
    tables = (ct, st, cc, sc, _overlap_t(T))
    x2 = x.reshape(B * T, D)
    for l in range(w_in.shape[0]):
        x2 = _layer(x2, B, T, norm_mix_pre[l], norm_mix_post[l], norm_ffn_pre[l], norm_ffn_post[l],
                    w_in[l], fox_forget_bias[l],
                    cmp_k_pe[l], cmp_k_w1[l], cmp_k_w2[l], cmp_v_pe[l], cmp_v_w1[l], cmp_v_w2[l],
                    w_up_fox[l], w_up_nsa[l], w_out[l], w_ffn_gate[l], w_ffn_up[l], w_ffn_down[l],
                    tables)
    return x2.reshape(B, T, D)
```

```python
import functools

import numpy as np
import jax
import jax.numpy as jnp
from jax import lax
from jax.experimental import pallas as pl
from jax.experimental.pallas import tpu as pltpu

D_MODEL = 2048
HEAD_DIM = 128
FOX_HEADS = 8
NSA_HEADS = 8
NSA_GROUP_SIZE = 4
NSA_KV_GROUPS = 2
FOX_W = FOX_HEADS * HEAD_DIM
NSA_W = NSA_HEADS * HEAD_DIM
NSA_KV_W = NSA_KV_GROUPS * HEAD_DIM
ROPE_DIM = HEAD_DIM // 4
ROPE_HALF = ROPE_DIM // 2
ROPE_THETA = 500000.0
CMP_LEN = 32
CMP_STRIDE = 16
CMP_HIDDEN = 256
SLC_LEN = 64
SLC_TOP = 16
WINDOW = 512
FFN_HIDDEN = 5632
EPS = 1e-6
NEG_INF = -1e30
FORCE_SCORE = 1e6
SCALE = HEAD_DIM ** -0.5
LOG2E = float(np.log2(np.e))
QSCALE = SCALE * LOG2E
IN_SIZES = [FOX_W, FOX_W, FOX_W, FOX_HEADS, NSA_W,
            NSA_KV_W, NSA_KV_W, NSA_KV_W, NSA_KV_W, NSA_KV_W, NSA_KV_W,
            NSA_HEADS * 3, D_MODEL, D_MODEL]

LANES = 128
MXU_DT = jnp.bfloat16
VMEM_LIMIT = 56 * 1024 * 1024

CB_FQ, CB_FK, CB_FV, CB_NQ = 0, 8, 16, 24
CB_NKC, CB_NVC, CB_NKS, CB_NVS, CB_NKW, CB_NVW = 32, 34, 36, 38, 40, 42
CB_GF, CB_GN, CB_SM = 44, 60, 76
NP_COLS = 77 * LANES
SM_FF, SM_GATE = 0, FOX_HEADS

F32 = jnp.float32


def _cparams(n_axes):
    return pltpu.CompilerParams(dimension_semantics=("arbitrary",) * n_axes,
                                vmem_limit_bytes=VMEM_LIMIT)


def _rms(x):
    return x * lax.rsqrt(jnp.mean(x * x, axis=-1, keepdims=True) + EPS)


def _dot(a, b):
    return jnp.dot(a, b, preferred_element_type=F32)


def _dot_nt(a, b):
    return lax.dot_general(a, b, (((1,), (1,)), ((), ())), preferred_element_type=F32)


def _split3(x):
    hi = x.astype(MXU_DT)
    r1 = x - hi.astype(F32)
    mid = r1.astype(MXU_DT)
    lo = (r1 - mid.astype(F32)).astype(MXU_DT)
    return hi, mid, lo


def _rope(x, c, s):
    lane = lax.broadcasted_iota(jnp.int32, x.shape, 1)
    partner = jnp.where(lane < ROPE_HALF,
                        pltpu.roll(x, LANES - ROPE_HALF, 1),
                        pltpu.roll(x, ROPE_HALF, 1))
    return x * c + partner * s


def _sigmoid(z):
    return 1.0 / (1.0 + jnp.exp(-z))


def _inproj_body(x_ref, g_ref, w_ref, o_ref, a_ref):
    @pl.when(pl.program_id(1) == 0)
    def _():
        a_ref[...] = (_rms(x_ref[...]) * g_ref[...]).astype(a_ref.dtype)

    o_ref[...] = _dot(a_ref[...], w_ref[...])


def _inproj(x2, g, w):
    m, d = x2.shape
    n = w.shape[1]
    tm = min(512, m)
    tn = 11 * LANES
    return pl.pallas_call(
        _inproj_body,
        out_shape=jax.ShapeDtypeStruct((m, n), F32),
        grid=(m // tm, n // tn),
        in_specs=[pl.BlockSpec((tm, d), lambda i, j: (i, 0)),
                  pl.BlockSpec((1, d), lambda i, j: (0, 0)),
                  pl.BlockSpec((d, tn), lambda i, j: (0, j))],
        out_specs=pl.BlockSpec((tm, tn), lambda i, j: (i, j)),
        scratch_shapes=[pltpu.VMEM((tm, d), MXU_DT)],
        compiler_params=_cparams(2),
        name="inproj",
    )(x2, g, w)


def _foxcum_body(s_ref, b_ref, cum_ref, *, T, CH):
    r = lax.broadcasted_iota(jnp.int32, (CH, CH), 0)
    c = lax.broadcasted_iota(jnp.int32, (CH, CH), 1)
    tri = jnp.where(c <= r, 1.0, 0.0).astype(MXU_DT)
    carry = jnp.zeros((1, LANES), F32)
    for ci in range(T // CH):
        z = s_ref[ci * CH:(ci + 1) * CH, :] + b_ref[...]
        lf = jnp.minimum(z, 0.0) - jnp.log1p(jnp.exp(-jnp.abs(z)))
        hi, mid, lo = _split3(lf)
        out = (_dot(tri, hi) + _dot(tri, mid)) + _dot(tri, lo) + carry
        cum_ref[ci * CH:(ci + 1) * CH, :] = out
        carry = out[CH - 1:CH, :]


def _foxcum(proj, bias_row, B, T):
    CH = 256
    return pl.pallas_call(
        functools.partial(_foxcum_body, T=T, CH=CH),
        out_shape=jax.ShapeDtypeStruct((B * T, LANES), F32),
        grid=(B,),
        in_specs=[pl.BlockSpec((T, LANES), lambda b: (b, CB_SM)),
                  pl.BlockSpec((1, LANES), lambda b: (0, 0))],
        out_specs=pl.BlockSpec((T, LANES), lambda b: (b, 0)),
        compiler_params=_cparams(1),
        name="foxcum",
    )(proj, bias_row)


def _online_softmax_step_t(st, vt_t, m_ref, l_ref, acc_ref):
    m_old = m_ref[...]
    m_new = jnp.maximum(m_old, jnp.max(st, axis=0, keepdims=True))
    alpha = jnp.exp2(m_old - m_new)
    p = jnp.exp2(st - m_new)
    l_ref[...] = alpha * l_ref[...] + jnp.sum(p, axis=0, keepdims=True)
    acc_ref[...] = alpha * acc_ref[...] + _dot(vt_t, p.astype(MXU_DT))
    m_ref[...] = m_new


def _fox_body(q_ref, k_ref, v_ref, cum_ref, o_ref,
              ka_ref, vt_ref, m_ref, l_ref, acc_ref, *, T, TQ):
    h = pl.program_id(1)
    i = pl.program_id(2)
    RC = 512

    def cum_pieces(rows, n):
        lane = lax.broadcasted_iota(jnp.int32, (n, LANES), 1)
        col = jnp.sum(jnp.where(lane == h, cum_ref[rows, :], 0.0), axis=1, keepdims=True) * LOG2E
        return [p.astype(F32) for p in _split3(col)]

    @pl.when(i == 0)
    def _():
        for c0 in range(0, T, RC):
            sl = slice(c0, c0 + RC)
            ka_ref[sl, 0:HEAD_DIM] = k_ref[sl, :].astype(MXU_DT)
            lane = lax.broadcasted_iota(jnp.int32, (RC, LANES), 1)
            aug = jnp.where((lane >= 3) & (lane < 6), 1.0, 0.0)
            for idx, c in enumerate(cum_pieces(sl, RC)):
                aug = jnp.where(lane == idx, c, aug)
            ka_ref[sl, HEAD_DIM:HEAD_DIM + LANES] = aug.astype(MXU_DT)
            vt_ref[:, sl] = v_ref[sl, :].T.astype(MXU_DT)

    q0 = pl.multiple_of(i * TQ, TQ)
    lane = lax.broadcasted_iota(jnp.int32, (TQ, LANES), 1)
    qaug = jnp.where(lane < 3, -1.0, 0.0)
    for idx, c in enumerate(cum_pieces(pl.ds(q0, TQ), TQ)):
        qaug = jnp.where(lane == 3 + idx, c, qaug)
    qa = jnp.concatenate([(q_ref[...] * QSCALE).astype(MXU_DT), qaug.astype(MXU_DT)], axis=1)
    m_ref[...] = jnp.full(m_ref.shape, NEG_INF, F32)
    l_ref[...] = jnp.zeros(l_ref.shape, F32)
    acc_ref[...] = jnp.zeros(acc_ref.shape, F32)

    def tile(k0, diagonal):
        st = _dot_nt(ka_ref[pl.ds(k0, TQ), :], qa)
        if diagonal:
            krow = lax.broadcasted_iota(jnp.int32, (TQ, TQ), 0)
            qcol = lax.broadcasted_iota(jnp.int32, (TQ, TQ), 1)
            st = jnp.where(krow <= qcol, st, NEG_INF)
        _online_softmax_step_t(st, vt_ref[:, pl.ds(k0, TQ)], m_ref, l_ref, acc_ref)

    def step(j, carry):
        tile(pl.multiple_of(j * TQ, TQ), False)
        return carry

    lax.fori_loop(0, i, step, 0)
    tile(q0, True)
    o_ref[...] = (acc_ref[...] / l_ref[...]).T.astype(o_ref.dtype)


def _fox(proj, cum, B, T):
    TQ = 512
    nq = T // TQ
    return pl.pallas_call(
        functools.partial(_fox_body, T=T, TQ=TQ),
        out_shape=jax.ShapeDtypeStruct((B * T, FOX_W), MXU_DT),
        grid=(B, FOX_HEADS, nq),
        in_specs=[pl.BlockSpec((TQ, LANES), lambda b, h, i: (b * nq + i, CB_FQ + h)),
                  pl.BlockSpec((T, LANES), lambda b, h, i: (b, CB_FK + h)),
                  pl.BlockSpec((T, LANES), lambda b, h, i: (b, CB_FV + h)),
                  pl.BlockSpec((T, LANES), lambda b, h, i: (b, 0))],
        out_specs=pl.BlockSpec((TQ, LANES), lambda b, h, i: (b * nq + i, h)),
        scratch_shapes=[pltpu.VMEM((T, HEAD_DIM + LANES), MXU_DT),
                        pltpu.VMEM((HEAD_DIM, T), MXU_DT),
                        pltpu.VMEM((1, TQ), F32),
                        pltpu.VMEM((1, TQ), F32),
                        pltpu.VMEM((HEAD_DIM, TQ), F32)],
        compiler_params=_cparams(3),
        name="fox",
    )(proj, proj, proj, cum)


def _gelu_tanh(x):
    return 0.5 * x * (1.0 + jnp.tanh(np.sqrt(2.0 / np.pi).astype(np.float32)
                                     * (x + 0.044715 * (x * x * x))))


def _compress_one(x_ref, pe_ref, w1_ref, w2_ref, NCP):
    h0 = jnp.zeros((NCP, CMP_HIDDEN), F32)
    h1 = jnp.zeros((NCP, CMP_HIDDEN), F32)
    for l in range(CMP_STRIDE):
        xl = x_ref[pl.ds(l, NCP, stride=CMP_STRIDE), :]
        a0 = (xl + pe_ref[l:l + 1, :]).astype(MXU_DT)
        a1 = (xl + pe_ref[CMP_STRIDE + l:CMP_STRIDE + l + 1, :]).astype(MXU_DT)
        h0 = h0 + _dot(a0, w1_ref[l * HEAD_DIM:(l + 1) * HEAD_DIM, :])
        h1 = h1 + _dot(a1, w1_ref[(CMP_STRIDE + l) * HEAD_DIM:(CMP_STRIDE + l + 1) * HEAD_DIM, :])
    hsum = h0 + pltpu.roll(h1, NCP - 1, 0)
    return _dot(_gelu_tanh(hsum).astype(MXU_DT), w2_ref[...])


def _compress_body(xk_ref, xv_ref, pek_ref, w1k_ref, w2k_ref, pev_ref, w1v_ref, w2v_ref,
                   cc_ref, sc_ref, kc_ref, vc_ref, *, NCP):
    kc = _compress_one(xk_ref, pek_ref, w1k_ref, w2k_ref, NCP)
    kc_ref[0, 0] = _rope(kc, cc_ref[...], sc_ref[...]).astype(kc_ref.dtype)
    vc = _compress_one(xv_ref, pev_ref, w1v_ref, w2v_ref, NCP)
    vc_ref[0, 0] = vc.T.astype(vc_ref.dtype)


def _compress(proj, pek, w1k, w2k, pev, w1v, w2v, cc, sc, B, T):
    NCP = T // CMP_STRIDE
    full = lambda shape: pl.BlockSpec(shape, lambda b, g: (0,) * len(shape))
    k_sds = jax.ShapeDtypeStruct((B, NSA_KV_GROUPS, NCP, HEAD_DIM), MXU_DT)
    v_sds = jax.ShapeDtypeStruct((B, NSA_KV_GROUPS, HEAD_DIM, NCP), MXU_DT)
    k_spec = pl.BlockSpec((1, 1, NCP, HEAD_DIM), lambda b, g: (b, g, 0, 0))
    v_spec = pl.BlockSpec((1, 1, HEAD_DIM, NCP), lambda b, g: (b, g, 0, 0))
    return pl.pallas_call(
        functools.partial(_compress_body, NCP=NCP),
        out_shape=(k_sds, v_sds),
        grid=(B, NSA_KV_GROUPS),
        in_specs=[pl.BlockSpec((T, LANES), lambda b, g: (b, CB_NKC + g)),
                  pl.BlockSpec((T, LANES), lambda b, g: (b, CB_NVC + g)),
                  full((CMP_LEN, HEAD_DIM)), full((CMP_LEN * HEAD_DIM, CMP_HIDDEN)),
                  full((CMP_HIDDEN, HEAD_DIM)),
                  full((CMP_LEN, HEAD_DIM)), full((CMP_LEN * HEAD_DIM, CMP_HIDDEN)),
                  full((CMP_HIDDEN, HEAD_DIM)),
                  full((NCP, LANES)), full((NCP, LANES))],
        out_specs=(k_spec, v_spec),
        compiler_params=_cparams(2),
        name="compress",
    )(proj, proj, pek, w1k, w2k, pev, w1v, w2v, cc, sc)


def _nsa_body(q_ref, sm_ref, cq_ref, sq_ref, kc_ref, vc_ref, ks_ref, vs_ref, kw_ref, vw_ref,
              ct_ref, st_ref, ovt_ref, o_ref,
              ksa_ref, vst_ref, kwb_ref, vwt_ref, smt_ref, m_ref, l_ref, acc_ref,
              *, T, TQ, TK, WSP, NB, NCP, NSEL):
    g = pl.program_id(1)
    i = pl.program_id(2)
    HG = NSA_GROUP_SIZE
    R = HG * TQ
    RC = 512

    @pl.when(i == 0)
    def _():
        for c0 in range(0, T, RC):
            sl = slice(c0, c0 + RC)
            ct = ct_ref[sl, :]
            st = st_ref[sl, :]
            ksa_ref[sl, 0:HEAD_DIM] = _rope(ks_ref[sl, :], ct, st).astype(MXU_DT)
            kblk = (c0 + lax.broadcasted_iota(jnp.int32, (RC, LANES), 0)) >> 6
            onehot = jnp.where(kblk == lax.broadcasted_iota(jnp.int32, (RC, LANES), 1), 1.0, 0.0)
            ksa_ref[sl, HEAD_DIM:HEAD_DIM + LANES] = onehot.astype(MXU_DT)
            kwb_ref[sl, :] = _rope(kw_ref[sl, :], ct, st).astype(MXU_DT)
            vst_ref[:, sl] = vs_ref[sl, :].T.astype(MXU_DT)
            vwt_ref[:, sl] = vw_ref[sl, :].T.astype(MXU_DT)

    q0 = i * TQ
    cq = cq_ref[...]
    sq = sq_ref[...]
    qb = jnp.concatenate(
        [_rope(q_ref[:, hh * HEAD_DIM:(hh + 1) * HEAD_DIM], cq, sq) * QSCALE for hh in range(HG)],
        axis=0).astype(MXU_DT)
    qpos = q0 + (lax.broadcasted_iota(jnp.int32, (1, R), 1) & (TQ - 1))

    sc = _dot_nt(kc_ref[0, 0], qb)
    nrow = lax.broadcasted_iota(jnp.int32, (NCP, R), 0)
    sc = jnp.where(nrow * CMP_STRIDE + (CMP_LEN - 1) <= qpos, sc, NEG_INF)
    pc = jnp.exp2(sc - jnp.max(sc, axis=0, keepdims=True))
    pc = pc / jnp.sum(pc, axis=0, keepdims=True)
    pc = pc * jnp.where(qpos >= CMP_LEN - 1, 1.0, 0.0)
    o_c = _dot(vc_ref[0, 0], pc.astype(MXU_DT))

    psum = (pc[:, 0:TQ] + pc[:, TQ:2 * TQ]) + (pc[:, 2 * TQ:3 * TQ] + pc[:, 3 * TQ:4 * TQ])
    hi, mid, lo = _split3(psum)
    ovt = ovt_ref[...]
    imp = (_dot(ovt, hi) + _dot(ovt, mid)) + _dot(ovt, lo)
    jblk = lax.broadcasted_iota(jnp.int32, (NB, TQ), 0)
    cur = (q0 + lax.broadcasted_iota(jnp.int32, (NB, TQ), 1)) >> 6
    forced = (jblk == 0) | (jblk == cur) | (jblk == cur - 1)
    imp = jnp.where(forced, FORCE_SCORE, jnp.where(jblk > cur, -1.0, imp))
    rank = jnp.zeros((NB, TQ), F32)
    for jp in range(NB):
        row = imp[jp:jp + 1, :]
        ge = jnp.where(row >= imp, 1.0, 0.0)
        gt = jnp.where(row > imp, 1.0, 0.0)
        rank = rank + jnp.where(jblk > jp, ge, gt)
    sel_t = jnp.where((rank < NSEL) & (imp >= 0.0), 1.0, 0.0)
    if NB < LANES:
        sel_t = jnp.concatenate([sel_t, jnp.zeros((LANES - NB, TQ), F32)], axis=0)
    selneg = jnp.where(sel_t.T > 0.5, 0.0, NEG_INF).astype(MXU_DT)
    qa = jnp.concatenate([qb, jnp.concatenate([selneg] * HG, axis=0)], axis=1)

    m_ref[...] = jnp.full(m_ref.shape, NEG_INF, F32)
    l_ref[...] = jnp.zeros(l_ref.shape, F32)
    acc_ref[...] = jnp.zeros(acc_ref.shape, F32)

    def tile(k0, causal):
        st = _dot_nt(ksa_ref[pl.ds(k0, TK), :], qa)
        if causal:
            kpos = k0 + lax.broadcasted_iota(jnp.int32, (TK, R), 0)
            st = jnp.where(kpos <= qpos, st, NEG_INF)
        _online_softmax_step_t(st, vst_ref[:, pl.ds(k0, TK)], m_ref, l_ref, acc_ref)

    def step(j, carry):
        tile(pl.multiple_of(j * TK, TK), False)
        return carry

    nk = ((i + 1) * TQ + TK - 1) // TK
    lax.fori_loop(0, nk - 1, step, 0)
    tile(pl.multiple_of((nk - 1) * TK, TK), True)
    o_s = acc_ref[...] / l_ref[...]

    w0 = pl.multiple_of(jnp.clip((i + 1) * TQ - WSP, 0, T - WSP), LANES)
    kt = kwb_ref[pl.ds(w0, WSP), :]
    vt_t = vwt_ref[:, pl.ds(w0, WSP)]
    dist = (q0 + lax.broadcasted_iota(jnp.int32, (WSP, TQ), 1)) \
        - (w0 + lax.broadcasted_iota(jnp.int32, (WSP, TQ), 0))
    wbias = jnp.where((dist >= 0) & (dist < WINDOW), 0.0, NEG_INF)

    smt_ref[...] = sm_ref[...].T
    for hh in range(HG):
        cols = slice(hh * TQ, (hh + 1) * TQ)
        sw = _dot_nt(kt, qb[cols]) + wbias
        pw = jnp.exp2(sw - jnp.max(sw, axis=0, keepdims=True))
        pw = pw / jnp.sum(pw, axis=0, keepdims=True)
        o_w = _dot(vt_t, pw.astype(MXU_DT))
        col = SM_GATE + 3 * (g * HG + hh)
        gate = [_sigmoid(smt_ref[pl.ds(col + c, 1), :]) for c in range(3)]
        o = gate[0] * o_c[:, cols] + gate[1] * o_s[:, cols] + gate[2] * o_w
        o_ref[:, hh * HEAD_DIM:(hh + 1) * HEAD_DIM] = o.T.astype(o_ref.dtype)


def _nsa(proj, kc, vc, ct, st, ovt, B, T):
    TQ = 128
    TK = 512
    WSP = WINDOW + TQ
    NB = T // SLC_LEN
    NCP = T // CMP_STRIDE
    nq = T // TQ
    HG = NSA_GROUP_SIZE
    qw = HG * HEAD_DIM
    kv_spec = lambda cb: pl.BlockSpec((T, LANES), lambda b, g, i: (b, cb + g))
    return pl.pallas_call(
        functools.partial(_nsa_body, T=T, TQ=TQ, TK=TK, WSP=WSP, NB=NB, NCP=NCP,
                          NSEL=min(SLC_TOP, NB)),
        out_shape=jax.ShapeDtypeStruct((B * T, NSA_W), MXU_DT),
        grid=(B, NSA_KV_GROUPS, nq),
        in_specs=[pl.BlockSpec((TQ, qw), lambda b, g, i: (b * nq + i, CB_NQ * LANES // qw + g)),
                  pl.BlockSpec((TQ, LANES), lambda b, g, i: (b * nq + i, CB_SM)),
                  pl.BlockSpec((TQ, LANES), lambda b, g, i: (i, 0)),
                  pl.BlockSpec((TQ, LANES), lambda b, g, i: (i, 0)),
                  pl.BlockSpec((1, 1, NCP, HEAD_DIM), lambda b, g, i: (b, g, 0, 0)),
                  pl.BlockSpec((1, 1, HEAD_DIM, NCP), lambda b, g, i: (b, g, 0, 0)),
                  kv_spec(CB_NKS), kv_spec(CB_NVS), kv_spec(CB_NKW), kv_spec(CB_NVW),
                  pl.BlockSpec((T, LANES), lambda b, g, i: (0, 0)),
                  pl.BlockSpec((T, LANES), lambda b, g, i: (0, 0)),
                  pl.BlockSpec((NB, NCP), lambda b, g, i: (0, 0))],
        out_specs=pl.BlockSpec((TQ, qw), lambda b, g, i: (b * nq + i, g)),
        scratch_shapes=[pltpu.VMEM((T, HEAD_DIM + LANES), MXU_DT),
                        pltpu.VMEM((HEAD_DIM, T), MXU_DT),
                        pltpu.VMEM((T, LANES), MXU_DT),
                        pltpu.VMEM((HEAD_DIM, T), MXU_DT),
                        pltpu.VMEM((LANES, TQ), F32),
                        pltpu.VMEM((1, HG * TQ), F32),
                        pltpu.VMEM((1, HG * TQ), F32),
                        pltpu.VMEM((HEAD_DIM, HG * TQ), F32)],
        compiler_params=_cparams(3),
        name="nsa",
    )(proj, proj, ct, st, kc, vc, proj, proj, proj, proj, ct, st, ovt)


def _merge_body(of_ref, on_ref, wf_ref, wn_ref, gf_ref, gn_ref, o_ref):
    yf = _dot(of_ref[...], wf_ref[...])
    yn = _dot(on_ref[...], wn_ref[...])
    o_ref[...] = (_sigmoid(gf_ref[...]) * yf + _sigmoid(gn_ref[...]) * yn).astype(o_ref.dtype)


def _merge(o_fox, o_nsa, wf, wn, proj):
    m = o_fox.shape[0]
    tm = min(512, m)
    tn = 512
    gfb = CB_GF * LANES // tn
    gnb = CB_GN * LANES // tn
    return pl.pallas_call(
        _merge_body,
        out_shape=jax.ShapeDtypeStruct((m, D_MODEL), MXU_DT),
        grid=(m // tm, D_MODEL // tn),
        in_specs=[pl.BlockSpec((tm, FOX_W), lambda i, j: (i, 0)),
                  pl.BlockSpec((tm, NSA_W), lambda i, j: (i, 0)),
                  pl.BlockSpec((FOX_W, tn), lambda i, j: (0, j)),
                  pl.BlockSpec((NSA_W, tn), lambda i, j: (0, j)),
                  pl.BlockSpec((tm, tn), lambda i, j: (i, gfb + j)),
                  pl.BlockSpec((tm, tn), lambda i, j: (i, gnb + j))],
        out_specs=pl.BlockSpec((tm, tn), lambda i, j: (i, j)),
        compiler_params=_cparams(2),
        name="merge",
    )(o_fox, o_nsa, wf, wn, proj, proj)


def _outproj_body(a_ref, w_ref, g_ref, x_ref, o_ref):
    y = _dot(a_ref[...], w_ref[...])
    o_ref[...] = x_ref[...] + _rms(y) * g_ref[...]


def _outproj(mix, w, g, x2):
    m, d = x2.shape
    tm = min(512, m)
    return pl.pallas_call(
        _outproj_body,
        out_shape=jax.ShapeDtypeStruct((m, d), F32),
        grid=(m // tm,),
        in_specs=[pl.BlockSpec((tm, d), lambda i: (i, 0)),
                  pl.BlockSpec((d, d), lambda i: (0, 0)),
                  pl.BlockSpec((1, d), lambda i: (0, 0)),
                  pl.BlockSpec((tm, d), lambda i: (i, 0))],
        out_specs=pl.BlockSpec((tm, d), lambda i: (i, 0)),
        compiler_params=_cparams(1),
        name="outproj",
    )(mix, w, g, x2)


def _ffn_up_body(x_ref, g_ref, wg_ref, wu_ref, o_ref, a_ref):
    @pl.when(pl.program_id(1) == 0)
    def _():
        a_ref[...] = (_rms(x_ref[...]) * g_ref[...]).astype(a_ref.dtype)

    a = a_ref[...]
    hg = _dot(a, wg_ref[...])
    hu = _dot(a, wu_ref[...])
    o_ref[...] = (hg * _sigmoid(hg) * hu).astype(o_ref.dtype)


def _ffn_up(x2, g, wg, wu):
    m, d = x2.shape
    f = wg.shape[1]
    tm = min(1024, m)
    tn = 512
    return pl.pallas_call(
        _ffn_up_body,
        out_shape=jax.ShapeDtypeStruct((m, f), MXU_DT),
        grid=(m // tm, f // tn),
        in_specs=[pl.BlockSpec((tm, d), lambda i, j: (i, 0)),
                  pl.BlockSpec((1, d), lambda i, j: (0, 0)),
                  pl.BlockSpec((d, tn), lambda i, j: (0, j)),
                  pl.BlockSpec((d, tn), lambda i, j: (0, j))],
        out_specs=pl.BlockSpec((tm, tn), lambda i, j: (i, j)),
        scratch_shapes=[pltpu.VMEM((tm, d), MXU_DT)],
        compiler_params=_cparams(2),
        name="ffn_up",
    )(x2, g, wg, wu)


def _ffn_down_body(h_ref, w_ref, g_ref, x_ref, o_ref, acc_ref):
    k = pl.program_id(1)

    @pl.when(k == 0)
    def _():
        acc_ref[...] = jnp.zeros(acc_ref.shape, F32)

    acc_ref[...] += _dot(h_ref[...], w_ref[...])

    @pl.when(k == pl.num_programs(1) - 1)
    def _():
        o_ref[...] = x_ref[...] + _rms(acc_ref[...]) * g_ref[...]


def _ffn_down(h, w, g, x2):
    m, d = x2.shape
    f = h.shape[1]
    tm = min(512, m)
    tk = 11 * LANES
    return pl.pallas_call(
        _ffn_down_body,
        out_shape=jax.ShapeDtypeStruct((m, d), F32),
        grid=(m // tm, f // tk),
        in_specs=[pl.BlockSpec((tm, tk), lambda i, k: (i, k)),
                  pl.BlockSpec((tk, d), lambda i, k: (k, 0)),
                  pl.BlockSpec((1, d), lambda i, k: (0, 0)),
                  pl.BlockSpec((tm, d), lambda i, k: (i, 0))],
        out_specs=pl.BlockSpec((tm, d), lambda i, k: (i, 0)),
        scratch_shapes=[pltpu.VMEM((tm, d), F32)],
        compiler_params=_cparams(2),
        name="ffn_down",
    )(h, w, g, x2)


def _pack_w_in(w):
    offs = np.cumsum([0] + IN_SIZES)
    seg = lambda k: w[:, offs[k]:offs[k + 1]]
    small = jnp.concatenate(
        [seg(3), seg(11), jnp.zeros((w.shape[0], LANES - FOX_HEADS - 3 * NSA_HEADS), w.dtype)], axis=1)
    cols = [seg(k) for k in (0, 1, 2, 4, 5, 6, 7, 8, 9, 10, 12, 13)] + [small]
    return jnp.concatenate(cols, axis=1).astype(MXU_DT)


def _rope_tables(pos):
    inv_freq = jnp.power(ROPE_THETA, -jnp.arange(ROPE_HALF, dtype=F32) * (2.0 / ROPE_DIM))
    ang = pos[:, None] * inv_freq[None, :]
    cos, sin = jnp.cos(ang), jnp.sin(ang)
    n = pos.shape[0]
    pad = HEAD_DIM - ROPE_DIM
    c = jnp.concatenate([cos, cos, jnp.ones((n, pad), F32)], axis=1)
    s = jnp.concatenate([-sin, sin, jnp.zeros((n, pad), F32)], axis=1)
    return c, s


def _overlap_t(T):
    nb, ncp = T // SLC_LEN, T // CMP_STRIDE
    sc = np.arange(ncp) * CMP_STRIDE
    ss = np.arange(nb) * SLC_LEN
    ov = np.minimum(sc[None, :] + CMP_LEN, ss[:, None] + SLC_LEN) - np.maximum(sc[None, :], ss[:, None])
    ov = np.clip(ov, 0, None) / CMP_LEN
    ov[:, ncp - 1] = 0.0
    return jnp.asarray(ov, dtype=MXU_DT)


def _layer(x2, B, T, n_mix_pre, n_mix_post, n_ffn_pre, n_ffn_post, w_in, f_bias,
           ck_pe, ck_w1, ck_w2, cv_pe, cv_w1, cv_w2, w_up_fox, w_up_nsa, w_out, w_gate, w_up, w_down,
           tables):
    ct, st, cc, sc, ovt = tables
    row = lambda v: v.reshape(1, -1).astype(F32)
    bf = lambda w: w.astype(MXU_DT)
    proj = _inproj(x2, row(n_mix_pre), _pack_w_in(w_in))
    bias_row = jnp.pad(f_bias.astype(F32), (0, LANES - FOX_HEADS)).reshape(1, LANES)
    cum = _foxcum(proj, bias_row, B, T)
    o_fox = _fox(proj, cum, B, T)
    kc, vc = _compress(proj, ck_pe, bf(ck_w1), bf(ck_w2), cv_pe, bf(cv_w1), bf(cv_w2), cc, sc, B, T)
    o_nsa = _nsa(proj, kc, vc, ct, st, ovt, B, T)
    mix = _merge(o_fox, o_nsa, bf(w_up_fox), bf(w_up_nsa), proj)
    x2 = _outproj(mix, bf(w_out), row(n_mix_post), x2)
    h = _ffn_up(x2, row(n_ffn_pre), bf(w_gate), bf(w_up))
    return _ffn_down(h, bf(w_down), row(n_ffn_post), x2)


@jax.jit
def kernel(x, norm_mix_pre, norm_mix_post, norm_ffn_pre, norm_ffn_post, w_in, fox_forget_bias, cmp_k_pe, cmp_k_w1, cmp_k_w2, cmp_v_pe, cmp_v_w1, cmp_v_w2, w_up_fox, w_up_nsa, w_out, w_ffn_gate, w_ffn_up, w_ffn_down):
    B, T, D = x.shape
    ct, st = _rope_tables(jnp.arange(T, dtype=F32))
    ncp = T // CMP_STRIDE
    cc, sc = _rope_tables((jnp.arange(ncp) * CMP_STRIDE + CMP_LEN - 1).astype(F32))
    tables = (ct, st, cc, sc, _overlap_t(T))
    x2 = x.reshape(B * T, D)
    for l in range(w_in.shape[0]):
        x2 = _layer(x2, B, T, norm_mix_pre[l], norm_mix_post[l], norm_ffn_pre[l], norm_ffn_post[l],
                    w_in[l], fox_forget_bias[l],
                    cmp_k_pe[l], cmp_k_w1[l], cmp_k_w2[l], cmp_v_pe[l], cmp_v_w1[l], cmp_v_w2[l],
                    w_up_fox[l], w_up_nsa[l], w_out[l], w_ffn_gate[l], w_ffn_up[l], w_ffn_down[l],
                    tables)
    return x2.reshape(B, T, D)
```

```python
import functools

import numpy as np
import jax
import jax.numpy as jnp
from jax import lax
from jax.experimental import pallas as pl
from jax.experimental.pallas import tpu as pltpu

D_MODEL = 2048
HEAD_DIM = 128
FOX_HEADS = 8
NSA_HEADS = 8
NSA_GROUP_SIZE = 4
NSA_KV_GROUPS = 2
FOX_W = FOX_HEADS * HEAD_DIM
NSA_W = NSA_HEADS * HEAD_DIM
NSA_KV_W = NSA_KV_GROUPS * HEAD_DIM
ROPE_DIM = HEAD_DIM // 4
ROPE_HALF = ROPE_DIM // 2
ROPE_THETA = 500000.0
CMP_LEN = 32
CMP_STRIDE = 16
CMP_HIDDEN = 256
SLC_LEN = 64
SLC_TOP = 16
WINDOW = 512
FFN_HIDDEN = 5632
EPS = 1e-6
NEG_INF = -1e30
FORCE_SCORE = 1e6
SCALE = HEAD_DIM ** -0.5
LOG2E = float(np.log2(np.e))
QSCALE = SCALE * LOG2E
IN_SIZES = [FOX_W, FOX_W, FOX_W, FOX_HEADS, NSA_W,
            NSA_KV_W, NSA_KV_W, NSA_KV_W, NSA_KV_W, NSA_KV_W, NSA_KV_W,
            NSA_HEADS * 3, D_MODEL, D_MODEL]

LANES = 128
MXU_DT = jnp.bfloat16
VMEM_LIMIT = 56 * 1024 * 1024

CB_FQ, CB_FK, CB_FV, CB_NQ = 0, 8, 16, 24
CB_NKC, CB_NVC, CB_NKS, CB_NVS, CB_NKW, CB_NVW = 32, 34, 36, 38, 40, 42
CB_GF, CB_GN, CB_SM = 44, 60, 76
NP_COLS = 77 * LANES
SM_FF, SM_GATE = 0, FOX_HEADS

F32 = jnp.float32


def _cparams(n_axes):
    return pltpu.CompilerParams(dimension_semantics=("arbitrary",) * n_axes,
                                vmem_limit_bytes=VMEM_LIMIT)


def _rms(x):
    return x * lax.rsqrt(jnp.mean(x * x, axis=-1, keepdims=True) + EPS)


def _dot(a, b):
    return jnp.dot(a, b, preferred_element_type=F32)


def _dot_nt(a, b):
    return lax.dot_general(a, b, (((1,), (1,)), ((), ())), preferred_element_type=F32)


def _split3(x):
    hi = x.astype(MXU_DT)
    r1 = x - hi.astype(F32)
    mid = r1.astype(MXU_DT)
    lo = (r1 - mid.astype(F32)).astype(MXU_DT)
    return hi, mid, lo


def _rope(x, c, s):
    lane = lax.broadcasted_iota(jnp.int32, x.shape, 1)
    partner = jnp.where(lane < ROPE_HALF,
                        pltpu.roll(x, LANES - ROPE_HALF, 1),
                        pltpu.roll(x, ROPE_HALF, 1))
    return x * c + partner * s


def _sigmoid(z):
    return 1.0 / (1.0 + jnp.exp(-z))


def _inproj_body(x_ref, g_ref, w_ref, o_ref, a_ref):
    @pl.when(pl.program_id(1) == 0)
    def _():
        a_ref[...] = (_rms(x_ref[...]) * g_ref[...]).astype(a_ref.dtype)

    o_ref[...] = _dot(a_ref[...], w_ref[...])


def _inproj(x2, g, w):
    m, d = x2.shape
    n = w.shape[1]
    tm = min(1024, m)
    tn = 11 * LANES
    return pl.pallas_call(
        _inproj_body,
        out_shape=jax.ShapeDtypeStruct((m, n), F32),
        grid=(m // tm, n // tn),
        in_specs=[pl.BlockSpec((tm, d), lambda i, j: (i, 0)),
                  pl.BlockSpec((1, d), lambda i, j: (0, 0)),
                  pl.BlockSpec((d, tn), lambda i, j: (0, j))],
        out_specs=pl.BlockSpec((tm, tn), lambda i, j: (i, j)),
        scratch_shapes=[pltpu.VMEM((tm, d), MXU_DT)],
        compiler_params=_cparams(2),
        name="inproj",
    )(x2, g, w)


def _foxcum_body(s_ref, b_ref, cum_ref, *, T, CH):
    r = lax.broadcasted_iota(jnp.int32, (CH, CH), 0)
    c = lax.broadcasted_iota(jnp.int32, (CH, CH), 1)
    tri = jnp.where(c <= r, 1.0, 0.0).astype(MXU_DT)
    carry = jnp.zeros((1, LANES), F32)
    for ci in range(T // CH):
        z = s_ref[ci * CH:(ci + 1) * CH, :] + b_ref[...]
        lf = jnp.minimum(z, 0.0) - jnp.log1p(jnp.exp(-jnp.abs(z)))
        hi, mid, lo = _split3(lf)
        out = (_dot(tri, hi) + _dot(tri, mid)) + _dot(tri, lo) + carry
        cum_ref[ci * CH:(ci + 1) * CH, :] = out
        carry = out[CH - 1:CH, :]


def _foxcum(proj, bias_row, B, T):
    CH = 256
    return pl.pallas_call(
        functools.partial(_foxcum_body, T=T, CH=CH),
        out_shape=jax.ShapeDtypeStruct((B * T, LANES), F32),
        grid=(B,),
        in_specs=[pl.BlockSpec((T, LANES), lambda b: (b, CB_SM)),
                  pl.BlockSpec((1, LANES), lambda b: (0, 0))],
        out_specs=pl.BlockSpec((T, LANES), lambda b: (b, 0)),
        compiler_params=_cparams(1),
        name="foxcum",
    )(proj, bias_row)


def _online_softmax_step_t(st, vt_t, m_ref, l_ref, acc_ref):
    m_old = m_ref[...]
    m_new = jnp.maximum(m_old, jnp.max(st, axis=0, keepdims=True))
    alpha = jnp.exp2(m_old - m_new)
    p = jnp.exp2(st - m_new)
    l_ref[...] = alpha * l_ref[...] + jnp.sum(p, axis=0, keepdims=True)
    acc_ref[...] = alpha * acc_ref[...] + _dot(vt_t, p.astype(MXU_DT))
    m_ref[...] = m_new


def _fox_body(q_ref, k_ref, v_ref, cum_ref, o_ref,
              ka_ref, vt_ref, m_ref, l_ref, acc_ref, sa_ref, sb_ref, *, T, TQ):
    h = pl.program_id(1)
    i = pl.program_id(2)
    RC = 512

    def cum_pieces(rows, n):
        lane = lax.broadcasted_iota(jnp.int32, (n, LANES), 1)
        col = jnp.sum(jnp.where(lane == h, cum_ref[rows, :], 0.0), axis=1, keepdims=True) * LOG2E
        return [p.astype(F32) for p in _split3(col)]

    @pl.when(i == 0)
    def _():
        for c0 in range(0, T, RC):
            sl = slice(c0, c0 + RC)
            ka_ref[sl, 0:HEAD_DIM] = k_ref[sl, :].astype(MXU_DT)
            lane = lax.broadcasted_iota(jnp.int32, (RC, LANES), 1)
            aug = jnp.where((lane >= 3) & (lane < 6), 1.0, 0.0)
            for idx, c in enumerate(cum_pieces(sl, RC)):
                aug = jnp.where(lane == idx, c, aug)
            ka_ref[sl, HEAD_DIM:HEAD_DIM + LANES] = aug.astype(MXU_DT)
            vt_ref[:, sl] = v_ref[sl, :].T.astype(MXU_DT)

    q0 = pl.multiple_of(i * TQ, TQ)
    lane = lax.broadcasted_iota(jnp.int32, (TQ, LANES), 1)
    qaug = jnp.where(lane < 3, -1.0, 0.0)
    for idx, c in enumerate(cum_pieces(pl.ds(q0, TQ), TQ)):
        qaug = jnp.where(lane == 3 + idx, c, qaug)
    qa = jnp.concatenate([(q_ref[...] * QSCALE).astype(MXU_DT), qaug.astype(MXU_DT)], axis=1)
    m_ref[...] = jnp.full(m_ref.shape, NEG_INF, F32)
    l_ref[...] = jnp.zeros(l_ref.shape, F32)
    acc_ref[...] = jnp.zeros(acc_ref.shape, F32)

    def scores(t, s_ref):
        s_ref[...] = _dot_nt(ka_ref[pl.ds(pl.multiple_of(t * TQ, TQ), TQ), :], qa)

    def consume(t, s_ref, diagonal):
        st = s_ref[...]
        if diagonal:
            krow = lax.broadcasted_iota(jnp.int32, (TQ, TQ), 0)
            qcol = lax.broadcasted_iota(jnp.int32, (TQ, TQ), 1)
            st = jnp.where(krow <= qcol, st, NEG_INF)
        _online_softmax_step_t(st, vt_ref[:, pl.ds(pl.multiple_of(t * TQ, TQ), TQ)], m_ref, l_ref, acc_ref)

    scores(0, sa_ref)

    def pair(jj, carry):
        scores(2 * jj + 1, sb_ref)
        consume(2 * jj, sa_ref, False)
        scores(2 * jj + 2, sa_ref)
        consume(2 * jj + 1, sb_ref, False)
        return carry

    lax.fori_loop(0, i // 2, pair, 0)

    @pl.when(i % 2 == 0)
    def _():
        consume(i, sa_ref, True)

    @pl.when(i % 2 == 1)
    def _():
        scores(i, sb_ref)
        consume(i - 1, sa_ref, False)
        consume(i, sb_ref, True)

    o_ref[...] = (acc_ref[...] / l_ref[...]).T.astype(o_ref.dtype)


def _fox(proj, cum, B, T):
    TQ = 512
    nq = T // TQ
    return pl.pallas_call(
        functools.partial(_fox_body, T=T, TQ=TQ),
        out_shape=jax.ShapeDtypeStruct((B * T, FOX_W), MXU_DT),
        grid=(B, FOX_HEADS, nq),
        in_specs=[pl.BlockSpec((TQ, LANES), lambda b, h, i: (b * nq + i, CB_FQ + h)),
                  pl.BlockSpec((T, LANES), lambda b, h, i: (b, CB_FK + h)),
                  pl.BlockSpec((T, LANES), lambda b, h, i: (b, CB_FV + h)),
                  pl.BlockSpec((T, LANES), lambda b, h, i: (b, 0))],
        out_specs=pl.BlockSpec((TQ, LANES), lambda b, h, i: (b * nq + i, h)),
        scratch_shapes=[pltpu.VMEM((T, HEAD_DIM + LANES), MXU_DT),
                        pltpu.VMEM((HEAD_DIM, T), MXU_DT),
                        pltpu.VMEM((1, TQ), F32),
                        pltpu.VMEM((1, TQ), F32),
                        pltpu.VMEM((HEAD_DIM, TQ), F32),
                        pltpu.VMEM((TQ, TQ), F32),
                        pltpu.VMEM((TQ, TQ), F32)],
        compiler_params=_cparams(3),
        name="fox",
    )(proj, proj, proj, cum)


def _gelu_tanh(x):
    return 0.5 * x * (1.0 + jnp.tanh(np.sqrt(2.0 / np.pi).astype(np.float32)
                                     * (x + 0.044715 * (x * x * x))))


def _compress_one(x_ref, pe_ref, w1_ref, w2_ref, NCP):
    h0 = jnp.zeros((NCP, CMP_HIDDEN), F32)
    h1 = jnp.zeros((NCP, CMP_HIDDEN), F32)
    for l in range(CMP_STRIDE):
        xl = x_ref[pl.ds(l, NCP, stride=CMP_STRIDE), :]
        a0 = (xl + pe_ref[l:l + 1, :]).astype(MXU_DT)
        a1 = (xl + pe_ref[CMP_STRIDE + l:CMP_STRIDE + l + 1, :]).astype(MXU_DT)
        h0 = h0 + _dot(a0, w1_ref[l * HEAD_DIM:(l + 1) * HEAD_DIM, :])
        h1 = h1 + _dot(a1, w1_ref[(CMP_STRIDE + l) * HEAD_DIM:(CMP_STRIDE + l + 1) * HEAD_DIM, :])
    hsum = h0 + pltpu.roll(h1, NCP - 1, 0)
    return _dot(_gelu_tanh(hsum).astype(MXU_DT), w2_ref[...])


def _compress_body(xk_ref, xv_ref, pek_ref, w1k_ref, w2k_ref, pev_ref, w1v_ref, w2v_ref,
                   cc_ref, sc_ref, kc_ref, vc_ref, *, NCP):
    kc = _compress_one(xk_ref, pek_ref, w1k_ref, w2k_ref, NCP)
    kc_ref[0, 0] = _rope(kc, cc_ref[...], sc_ref[...]).astype(kc_ref.dtype)
    vc = _compress_one(xv_ref, pev_ref, w1v_ref, w2v_ref, NCP)
    vc_ref[0, 0] = vc.T.astype(vc_ref.dtype)


def _compress(proj, pek, w1k, w2k, pev, w1v, w2v, cc, sc, B, T):
    NCP = T // CMP_STRIDE
    full = lambda shape: pl.BlockSpec(shape, lambda b, g: (0,) * len(shape))
    k_sds = jax.ShapeDtypeStruct((B, NSA_KV_GROUPS, NCP, HEAD_DIM), MXU_DT)
    v_sds = jax.ShapeDtypeStruct((B, NSA_KV_GROUPS, HEAD_DIM, NCP), MXU_DT)
    k_spec = pl.BlockSpec((1, 1, NCP, HEAD_DIM), lambda b, g: (b, g, 0, 0))
    v_spec = pl.BlockSpec((1, 1, HEAD_DIM, NCP), lambda b, g: (b, g, 0, 0))
    return pl.pallas_call(
        functools.partial(_compress_body, NCP=NCP),
        out_shape=(k_sds, v_sds),
        grid=(B, NSA_KV_GROUPS),
        in_specs=[pl.BlockSpec((T, LANES), lambda b, g: (b, CB_NKC + g)),
                  pl.BlockSpec((T, LANES), lambda b, g: (b, CB_NVC + g)),
                  full((CMP_LEN, HEAD_DIM)), full((CMP_LEN * HEAD_DIM, CMP_HIDDEN)),
                  full((CMP_HIDDEN, HEAD_DIM)),
                  full((CMP_LEN, HEAD_DIM)), full((CMP_LEN * HEAD_DIM, CMP_HIDDEN)),
                  full((CMP_HIDDEN, HEAD_DIM)),
                  full((NCP, LANES)), full((NCP, LANES))],
        out_specs=(k_spec, v_spec),
        compiler_params=_cparams(2),
        name="compress",
    )(proj, proj, pek, w1k, w2k, pev, w1v, w2v, cc, sc)


def _nsa_body(q_ref, sm_ref, cq_ref, sq_ref, kc_ref, vc_ref, ks_ref, vs_ref, kw_ref, vw_ref,
              ct_ref, st_ref, ovt_ref, o_ref,
              ksa_ref, vst_ref, kwb_ref, vwt_ref, smt_ref, m_ref, l_ref, acc_ref, sa_ref, sb_ref,
              part_ref, *, T, TQ, TK, WSP, NB, NCP, NSEL):
    g = pl.program_id(1)
    i = pl.program_id(2)
    HG = NSA_GROUP_SIZE
    R = HG * TQ
    RC = 512

    @pl.when(i == 0)
    def _():
        for c0 in range(0, T, RC):
            sl = slice(c0, c0 + RC)
            ct = ct_ref[sl, :]
            st = st_ref[sl, :]
            ksa_ref[sl, 0:HEAD_DIM] = _rope(ks_ref[sl, :], ct, st).astype(MXU_DT)
            kblk = (c0 + lax.broadcasted_iota(jnp.int32, (RC, LANES), 0)) >> 6
            onehot = jnp.where(kblk == lax.broadcasted_iota(jnp.int32, (RC, LANES), 1), 1.0, 0.0)
            ksa_ref[sl, HEAD_DIM:HEAD_DIM + LANES] = onehot.astype(MXU_DT)
            kwb_ref[sl, :] = _rope(kw_ref[sl, :], ct, st).astype(MXU_DT)
            vst_ref[:, sl] = vs_ref[sl, :].T.astype(MXU_DT)
            vwt_ref[:, sl] = vw_ref[sl, :].T.astype(MXU_DT)

    q0 = i * TQ
    cq = cq_ref[...]
    sq = sq_ref[...]
    qb = jnp.concatenate(
        [_rope(q_ref[:, hh * HEAD_DIM:(hh + 1) * HEAD_DIM], cq, sq) * QSCALE for hh in range(HG)],
        axis=0).astype(MXU_DT)
    qpos = q0 + (lax.broadcasted_iota(jnp.int32, (1, R), 1) & (TQ - 1))

    sc = _dot_nt(kc_ref[0, 0], qb)
    nrow = lax.broadcasted_iota(jnp.int32, (NCP, R), 0)
    last_valid = (qpos - (CMP_LEN - 1)) >> 4
    sc = jnp.where(nrow <= last_valid, sc, NEG_INF)
    pc = jnp.exp2(sc - jnp.max(sc, axis=0, keepdims=True))
    pc = pc * (jnp.where(qpos >= CMP_LEN - 1, 1.0, 0.0) / jnp.sum(pc, axis=0, keepdims=True))
    o_c = _dot(vc_ref[0, 0], pc.astype(MXU_DT))

    smt_ref[...] = sm_ref[...].T

    def gate(hh, c):
        return _sigmoid(smt_ref[pl.ds(SM_GATE + 3 * (g * HG + hh) + c, 1), :])

    w0 = pl.multiple_of(jnp.clip((i + 1) * TQ - WSP, 0, T - WSP), LANES)
    kt = kwb_ref[pl.ds(w0, WSP), :]
    vt_t = vwt_ref[:, pl.ds(w0, WSP)]
    dist = (q0 + lax.broadcasted_iota(jnp.int32, (WSP, TQ), 1)) \
        - (w0 + lax.broadcasted_iota(jnp.int32, (WSP, TQ), 0))
    wbias = jnp.where((dist >= 0) & (dist < WINDOW), 0.0, NEG_INF)
    for hh in range(HG):
        cols = slice(hh * TQ, (hh + 1) * TQ)
        sw = _dot_nt(kt, qb[cols]) + wbias
        pw = jnp.exp2(sw - jnp.max(sw, axis=0, keepdims=True))
        o_w = _dot(vt_t, pw.astype(MXU_DT)) / jnp.sum(pw, axis=0, keepdims=True)
        part_ref[:, cols] = gate(hh, 0) * o_c[:, cols] + gate(hh, 2) * o_w

    psum = (pc[:, 0:TQ] + pc[:, TQ:2 * TQ]) + (pc[:, 2 * TQ:3 * TQ] + pc[:, 3 * TQ:4 * TQ])
    hi, mid, lo = _split3(psum)
    ovt = ovt_ref[...]
    imp = (_dot(ovt, hi) + _dot(ovt, mid)) + _dot(ovt, lo)
    jblk = lax.broadcasted_iota(jnp.int32, (NB, TQ), 0)
    cur = (q0 + lax.broadcasted_iota(jnp.int32, (NB, TQ), 1)) >> 6
    forced = (jblk == 0) | (jblk == cur) | (jblk == cur - 1)
    imp = jnp.where(forced, FORCE_SCORE, jnp.where(jblk > cur, -1.0, imp))
    SUB = 8
    groups = [imp[r * SUB:(r + 1) * SUB, :] for r in range(NB // SUB)]
    sub = lax.broadcasted_iota(jnp.int32, (SUB, TQ), 0)
    ranks = [jnp.zeros((SUB, TQ), F32) for _ in groups]
    for jp in range(NB):
        gj, sj = divmod(jp, SUB)
        row = groups[gj][sj:sj + 1, :]
        for r, grp in enumerate(groups):
            if r > gj:
                beats = jnp.where(row >= grp, 1.0, 0.0)
            elif r < gj:
                beats = jnp.where(row > grp, 1.0, 0.0)
            else:
                beats = jnp.where(sub > sj, jnp.where(row >= grp, 1.0, 0.0), jnp.where(row > grp, 1.0, 0.0))
            ranks[r] = ranks[r] + beats
    rank = jnp.concatenate(ranks, axis=0)
    sel_t = jnp.where((rank < NSEL) & (imp >= 0.0), 1.0, 0.0)
    if NB < LANES:
        sel_t = jnp.concatenate([sel_t, jnp.zeros((LANES - NB, TQ), F32)], axis=0)
    selneg = jnp.where(sel_t.T > 0.5, 0.0, NEG_INF).astype(MXU_DT)
    qa = jnp.concatenate([qb, jnp.concatenate([selneg] * HG, axis=0)], axis=1)

    m_ref[...] = jnp.full(m_ref.shape, NEG_INF, F32)
    l_ref[...] = jnp.zeros(l_ref.shape, F32)
    acc_ref[...] = jnp.zeros(acc_ref.shape, F32)

    def scores(t, s_ref):
        s_ref[...] = _dot_nt(ksa_ref[pl.ds(pl.multiple_of(t * TK, TK), TK), :], qa)

    def consume(t, s_ref, causal):
        st = s_ref[...]
        k0 = pl.multiple_of(t * TK, TK)
        if causal:
            kpos = k0 + lax.broadcasted_iota(jnp.int32, (TK, R), 0)
            st = jnp.where(kpos <= qpos, st, NEG_INF)
        _online_softmax_step_t(st, vst_ref[:, pl.ds(k0, TK)], m_ref, l_ref, acc_ref)

    last = ((i + 1) * TQ + TK - 1) // TK - 1
    scores(0, sa_ref)

    def pair(jj, carry):
        scores(2 * jj + 1, sb_ref)
        consume(2 * jj, sa_ref, False)
        scores(2 * jj + 2, sa_ref)
        consume(2 * jj + 1, sb_ref, False)
        return carry

    lax.fori_loop(0, last // 2, pair, 0)

    @pl.when(last % 2 == 0)
    def _():
        consume(last, sa_ref, True)

    @pl.when(last % 2 == 1)
    def _():
        scores(last, sb_ref)
        consume(last - 1, sa_ref, False)
        consume(last, sb_ref, True)

    o_s = acc_ref[...] / l_ref[...]
    for hh in range(HG):
        cols = slice(hh * TQ, (hh + 1) * TQ)
        o = part_ref[:, cols] + gate(hh, 1) * o_s[:, cols]
        o_ref[:, hh * HEAD_DIM:(hh + 1) * HEAD_DIM] = o.T.astype(o_ref.dtype)


def _nsa(proj, kc, vc, ct, st, ovt, B, T):
    TQ = 128
    TK = 512
    WSP = WINDOW + TQ
    NB = T // SLC_LEN
    NCP = T // CMP_STRIDE
    nq = T // TQ
    HG = NSA_GROUP_SIZE
    qw = HG * HEAD_DIM
    kv_spec = lambda cb: pl.BlockSpec((T, LANES), lambda b, g, i: (b, cb + g))
    return pl.pallas_call(
        functools.partial(_nsa_body, T=T, TQ=TQ, TK=TK, WSP=WSP, NB=NB, NCP=NCP,
                          NSEL=min(SLC_TOP, NB)),
        out_shape=jax.ShapeDtypeStruct((B * T, NSA_W), MXU_DT),
        grid=(B, NSA_KV_GROUPS, nq),
        in_specs=[pl.BlockSpec((TQ, qw), lambda b, g, i: (b * nq + i, CB_NQ * LANES // qw + g)),
                  pl.BlockSpec((TQ, LANES), lambda b, g, i: (b * nq + i, CB_SM)),
                  pl.BlockSpec((TQ, LANES), lambda b, g, i: (i, 0)),
                  pl.BlockSpec((TQ, LANES), lambda b, g, i: (i, 0)),
                  pl.BlockSpec((1, 1, NCP, HEAD_DIM), lambda b, g, i: (b, g, 0, 0)),
                  pl.BlockSpec((1, 1, HEAD_DIM, NCP), lambda b, g, i: (b, g, 0, 0)),
                  kv_spec(CB_NKS), kv_spec(CB_NVS), kv_spec(CB_NKW), kv_spec(CB_NVW),
                  pl.BlockSpec((T, LANES), lambda b, g, i: (0, 0)),
                  pl.BlockSpec((T, LANES), lambda b, g, i: (0, 0)),
                  pl.BlockSpec((NB, NCP), lambda b, g, i: (0, 0))],
        out_specs=pl.BlockSpec((TQ, qw), lambda b, g, i: (b * nq + i, g)),
        scratch_shapes=[pltpu.VMEM((T, HEAD_DIM + LANES), MXU_DT),
                        pltpu.VMEM((HEAD_DIM, T), MXU_DT),
                        pltpu.VMEM((T, LANES), MXU_DT),
                        pltpu.VMEM((HEAD_DIM, T), MXU_DT),
                        pltpu.VMEM((LANES, TQ), F32),
                        pltpu.VMEM((1, HG * TQ), F32),
                        pltpu.VMEM((1, HG * TQ), F32),
                        pltpu.VMEM((HEAD_DIM, HG * TQ), F32),
                        pltpu.VMEM((TK, HG * TQ), F32),
                        pltpu.VMEM((TK, HG * TQ), F32),
                        pltpu.VMEM((HEAD_DIM, HG * TQ), F32)],
        compiler_params=_cparams(3),
        name="nsa",
    )(proj, proj, ct, st, kc, vc, proj, proj, proj, proj, ct, st, ovt)


def _merge_body(of_ref, on_ref, wf_ref, wn_ref, gf_ref, gn_ref, o_ref):
    yf = _dot(of_ref[...], wf_ref[...])
    yn = _dot(on_ref[...], wn_ref[...])
    o_ref[...] = (_sigmoid(gf_ref[...]) * yf + _sigmoid(gn_ref[...]) * yn).astype(o_ref.dtype)


def _merge(o_fox, o_nsa, wf, wn, proj):
    m = o_fox.shape[0]
    tm = min(1024, m)
    tn = 512
    gfb = CB_GF * LANES // tn
    gnb = CB_GN * LANES // tn
    return pl.pallas_call(
        _merge_body,
        out_shape=jax.ShapeDtypeStruct((m, D_MODEL), MXU_DT),
        grid=(m // tm, D_MODEL // tn),
        in_specs=[pl.BlockSpec((tm, FOX_W), lambda i, j: (i, 0)),
                  pl.BlockSpec((tm, NSA_W), lambda i, j: (i, 0)),
                  pl.BlockSpec((FOX_W, tn), lambda i, j: (0, j)),
                  pl.BlockSpec((NSA_W, tn), lambda i, j: (0, j)),
                  pl.BlockSpec((tm, tn), lambda i, j: (i, gfb + j)),
                  pl.BlockSpec((tm, tn), lambda i, j: (i, gnb + j))],
        out_specs=pl.BlockSpec((tm, tn), lambda i, j: (i, j)),
        compiler_params=_cparams(2),
        name="merge",
    )(o_fox, o_nsa, wf, wn, proj, proj)


def _outproj_body(a_ref, w_ref, g_ref, x_ref, o_ref):
    y = _dot(a_ref[...], w_ref[...])
    o_ref[...] = x_ref[...] + _rms(y) * g_ref[...]


def _outproj(mix, w, g, x2):
    m, d = x2.shape
    tm = min(512, m)
    return pl.pallas_call(
        _outproj_body,
        out_shape=jax.ShapeDtypeStruct((m, d), F32),
        grid=(m // tm,),
        in_specs=[pl.BlockSpec((tm, d), lambda i: (i, 0)),
                  pl.BlockSpec((d, d), lambda i: (0, 0)),
                  pl.BlockSpec((1, d), lambda i: (0, 0)),
                  pl.BlockSpec((tm, d), lambda i: (i, 0))],
        out_specs=pl.BlockSpec((tm, d), lambda i: (i, 0)),
        compiler_params=_cparams(1),
        name="outproj",
    )(mix, w, g, x2)


def _ffn_up_body(x_ref, g_ref, wg_ref, wu_ref, o_ref, a_ref):
    @pl.when(pl.program_id(1) == 0)
    def _():
        a_ref[...] = (_rms(x_ref[...]) * g_ref[...]).astype(a_ref.dtype)

    a = a_ref[...]
    hg = _dot(a, wg_ref[...])
    hu = _dot(a, wu_ref[...])
    o_ref[...] = (hg * _sigmoid(hg) * hu).astype(o_ref.dtype)


def _ffn_up(x2, g, wg, wu):
    m, d = x2.shape
    f = wg.shape[1]
    tm = min(1024, m)
    tn = 512
    return pl.pallas_call(
        _ffn_up_body,
        out_shape=jax.ShapeDtypeStruct((m, f), MXU_DT),
        grid=(m // tm, f // tn),
        in_specs=[pl.BlockSpec((tm, d), lambda i, j: (i, 0)),
                  pl.BlockSpec((1, d), lambda i, j: (0, 0)),
                  pl.BlockSpec((d, tn), lambda i, j: (0, j)),
                  pl.BlockSpec((d, tn), lambda i, j: (0, j))],
        out_specs=pl.BlockSpec((tm, tn), lambda i, j: (i, j)),
        scratch_shapes=[pltpu.VMEM((tm, d), MXU_DT)],
        compiler_params=_cparams(2),
        name="ffn_up",
    )(x2, g, wg, wu)


def _ffn_down_body(h_ref, w_ref, g_ref, x_ref, o_ref, acc_ref):
    k = pl.program_id(1)

    @pl.when(k == 0)
    def _():
        acc_ref[...] = jnp.zeros(acc_ref.shape, F32)

    acc_ref[...] += _dot(h_ref[...], w_ref[...])

    @pl.when(k == pl.num_programs(1) - 1)
    def _():
        o_ref[...] = x_ref[...] + _rms(acc_ref[...]) * g_ref[...]


def _ffn_down(h, w, g, x2):
    m, d = x2.shape
    f = h.shape[1]
    tm = min(512, m)
    tk = 11 * LANES
    return pl.pallas_call(
        _ffn_down_body,
        out_shape=jax.ShapeDtypeStruct((m, d), F32),
        grid=(m // tm, f // tk),
        in_specs=[pl.BlockSpec((tm, tk), lambda i, k: (i, k)),
                  pl.BlockSpec((tk, d), lambda i, k: (k, 0)),
                  pl.BlockSpec((1, d), lambda i, k: (0, 0)),
                  pl.BlockSpec((tm, d), lambda i, k: (i, 0))],
        out_specs=pl.BlockSpec((tm, d), lambda i, k: (i, 0)),
        scratch_shapes=[pltpu.VMEM((tm, d), F32)],
        compiler_params=_cparams(2),
        name="ffn_down",
    )(h, w, g, x2)


def _pack_w_in(w):
    offs = np.cumsum([0] + IN_SIZES)
    seg = lambda k: w[:, offs[k]:offs[k + 1]]
    small = jnp.concatenate(
        [seg(3), seg(11), jnp.zeros((w.shape[0], LANES - FOX_HEADS - 3 * NSA_HEADS), w.dtype)], axis=1)
    cols = [w[:, offs[0]:offs[3]], w[:, offs[4]:offs[11]], w[:, offs[12]:offs[14]], small]
    return jnp.concatenate(cols, axis=1).astype(MXU_DT)


def _rope_tables(pos):
    inv_freq = jnp.power(ROPE_THETA, -jnp.arange(ROPE_HALF, dtype=F32) * (2.0 / ROPE_DIM))
    ang = pos[:, None] * inv_freq[None, :]
    cos, sin = jnp.cos(ang), jnp.sin(ang)
    n = pos.shape[0]
    pad = HEAD_DIM - ROPE_DIM
    c = jnp.concatenate([cos, cos, jnp.ones((n, pad), F32)], axis=1)
    s = jnp.concatenate([-sin, sin, jnp.zeros((n, pad), F32)], axis=1)
    return c, s


def _overlap_t(T):
    nb, ncp = T // SLC_LEN, T // CMP_STRIDE
    sc = np.arange(ncp) * CMP_STRIDE
    ss = np.arange(nb) * SLC_LEN
    ov = np.minimum(sc[None, :] + CMP_LEN, ss[:, None] + SLC_LEN) - np.maximum(sc[None, :], ss[:, None])
    ov = np.clip(ov, 0, None) / CMP_LEN
    ov[:, ncp - 1] = 0.0
    return jnp.asarray(ov, dtype=MXU_DT)


def _layer(x2, B, T, n_mix_pre, n_mix_post, n_ffn_pre, n_ffn_post, w_in, f_bias,
           ck_pe, ck_w1, ck_w2, cv_pe, cv_w1, cv_w2, w_up_fox, w_up_nsa, w_out, w_gate, w_up, w_down,
           tables):
    ct, st, cc, sc, ovt = tables
    row = lambda v: v.reshape(1, -1).astype(F32)
    bf = lambda w: w.astype(MXU_DT)
    proj = _inproj(x2, row(n_mix_pre), _pack_w_in(w_in))
    bias_row = jnp.pad(f_bias.astype(F32), (0, LANES - FOX_HEADS)).reshape(1, LANES)
    cum = _foxcum(proj, bias_row, B, T)
    o_fox = _fox(proj, cum, B, T)
    kc, vc = _compress(proj, ck_pe, bf(ck_w1), bf(ck_w2), cv_pe, bf(cv_w1), bf(cv_w2), cc, sc, B, T)
    o_nsa = _nsa(proj, kc, vc, ct, st, ovt, B, T)
    mix = _merge(o_fox, o_nsa, bf(w_up_fox), bf(w_up_nsa), proj)
    x2 = _outproj(mix, bf(w_out), row(n_mix_post), x2)
    h = _ffn_up(x2, row(n_ffn_pre), bf(w_gate), bf(w_up))
    return _ffn_down(h, bf(w_down), row(n_ffn_post), x2)


@jax.jit
def kernel(x, norm_mix_pre, norm_mix_post, norm_ffn_pre, norm_ffn_post, w_in, fox_forget_bias, cmp_k_pe, cmp_k_w1, cmp_k_w2, cmp_v_pe, cmp_v_w1, cmp_v_w2, w_up_fox, w_up_nsa, w_out, w_ffn_gate, w_ffn_up, w_ffn_down):
    B, T, D = x.shape
    ct, st = _rope_tables(jnp.arange(T, dtype=F32))
    ncp = T // CMP_STRIDE
    cc, sc = _rope_tables((jnp.arange(ncp) * CMP_STRIDE + CMP_LEN - 1).astype(F32))
    tables = (ct, st, cc, sc, _overlap_t(T))
    x2 = x.reshape(B * T, D)
    for l in range(w_in.shape[0]):
        x2 = _layer(x2, B, T, norm_mix_pre[l], norm_mix_post[l], norm_ffn_pre[l], norm_ffn_post[l],
                    w_in[l], fox_forget_bias[l],
                    cmp_k_pe[l], cmp_k_w1[l], cmp_k_w2[l], cmp_v_pe[l], cmp_v_w1[l], cmp_v_w2[l],
                    w_up_fox[l], w_up_nsa[l], w_out[l], w_ffn_gate[l], w_ffn_up[l], w_ffn_down[l],
                    tables)
    return x2.reshape(B, T, D)
```

```python
import functools

import numpy as np
import jax
import jax.numpy as jnp
from jax import lax
from jax.experimental import pallas as pl
from jax.experimental.pallas import tpu as pltpu

D_MODEL = 2048
HEAD_DIM = 128
FOX_HEADS = 8
NSA_HEADS = 8
NSA_GROUP_SIZE = 4
NSA_KV_GROUPS = 2
FOX_W = FOX_HEADS * HEAD_DIM
NSA_W = NSA_HEADS * HEAD_DIM
NSA_KV_W = NSA_KV_GROUPS * HEAD_DIM
ROPE_DIM = HEAD_DIM // 4
ROPE_HALF = ROPE_DIM // 2
ROPE_THETA = 500000.0
CMP_LEN = 32
CMP_STRIDE = 16
CMP_HIDDEN = 256
SLC_LEN = 64
SLC_TOP = 16
WINDOW = 512
FFN_HIDDEN = 5632
EPS = 1e-6
NEG_INF = -1e30
FORCE_SCORE = 1e6
SCALE = HEAD_DIM ** -0.5
LOG2E = float(np.log2(np.e))
QSCALE = SCALE * LOG2E
IN_SIZES = [FOX_W, FOX_W, FOX_W, FOX_HEADS, NSA_W,
            NSA_KV_W, NSA_KV_W, NSA_KV_W, NSA_KV_W, NSA_KV_W, NSA_KV_W,
            NSA_HEADS * 3, D_MODEL, D_MODEL]

LANES = 128
MXU_DT = jnp.bfloat16
VMEM_LIMIT = 56 * 1024 * 1024

CB_FQ, CB_FK, CB_FV, CB_NQ = 0, 8, 16, 24
CB_NKC, CB_NVC, CB_NKS, CB_NVS, CB_NKW, CB_NVW = 32, 34, 36, 38, 40, 42
CB_GF, CB_GN, CB_SM = 44, 60, 76
NP_COLS = 77 * LANES
SM_FF, SM_GATE = 0, FOX_HEADS

F32 = jnp.float32


def _cparams(n_axes):
    return pltpu.CompilerParams(dimension_semantics=("arbitrary",) * n_axes,
                                vmem_limit_bytes=VMEM_LIMIT)


def _rms(x):
    return x * lax.rsqrt(jnp.mean(x * x, axis=-1, keepdims=True) + EPS)


def _dot(a, b):
    return jnp.dot(a, b, preferred_element_type=F32)


def _dot_nt(a, b):
    return lax.dot_general(a, b, (((1,), (1,)), ((), ())), preferred_element_type=F32)


def _split3(x):
    hi = x.astype(MXU_DT)
    r1 = x - hi.astype(F32)
    mid = r1.astype(MXU_DT)
    lo = (r1 - mid.astype(F32)).astype(MXU_DT)
    return hi, mid, lo


def _rope(x, c, s):
    lane = lax.broadcasted_iota(jnp.int32, x.shape, 1)
    partner = jnp.where(lane < ROPE_HALF,
                        pltpu.roll(x, LANES - ROPE_HALF, 1),
                        pltpu.roll(x, ROPE_HALF, 1))
    return x * c + partner * s


def _sigmoid(z):
    return 1.0 / (1.0 + jnp.exp(-z))


def _inproj_body(x_ref, g_ref, w_ref, o_ref, a_ref):
    @pl.when(pl.program_id(1) == 0)
    def _():
        a_ref[...] = (_rms(x_ref[...]) * g_ref[...]).astype(a_ref.dtype)

    o_ref[...] = _dot(a_ref[...], w_ref[...])


def _inproj(x2, g, w):
    m, d = x2.shape
    n = w.shape[1]
    tm = min(1024, m)
    tn = 11 * LANES
    return pl.pallas_call(
        _inproj_body,
        out_shape=jax.ShapeDtypeStruct((m, n), F32),
        grid=(m // tm, n // tn),
        in_specs=[pl.BlockSpec((tm, d), lambda i, j: (i, 0)),
                  pl.BlockSpec((1, d), lambda i, j: (0, 0)),
                  pl.BlockSpec((d, tn), lambda i, j: (0, j))],
        out_specs=pl.BlockSpec((tm, tn), lambda i, j: (i, j)),
        scratch_shapes=[pltpu.VMEM((tm, d), MXU_DT)],
        compiler_params=_cparams(2),
        name="inproj",
    )(x2, g, w)


def _foxcum_body(s_ref, b_ref, cum_ref, *, T, CH):
    r = lax.broadcasted_iota(jnp.int32, (CH, CH), 0)
    c = lax.broadcasted_iota(jnp.int32, (CH, CH), 1)
    tri = jnp.where(c <= r, 1.0, 0.0).astype(MXU_DT)
    carry = jnp.zeros((1, LANES), F32)
    for ci in range(T // CH):
        z = s_ref[ci * CH:(ci + 1) * CH, :] + b_ref[...]
        lf = jnp.minimum(z, 0.0) - jnp.log1p(jnp.exp(-jnp.abs(z)))
        hi, mid, lo = _split3(lf)
        out = (_dot(tri, hi) + _dot(tri, mid)) + _dot(tri, lo) + carry
        cum_ref[ci * CH:(ci + 1) * CH, :] = out
        carry = out[CH - 1:CH, :]


def _foxcum(proj, bias_row, B, T):
    CH = 256
    return pl.pallas_call(
        functools.partial(_foxcum_body, T=T, CH=CH),
        out_shape=jax.ShapeDtypeStruct((B * T, LANES), F32),
        grid=(B,),
        in_specs=[pl.BlockSpec((T, LANES), lambda b: (b, CB_SM)),
                  pl.BlockSpec((1, LANES), lambda b: (0, 0))],
        out_specs=pl.BlockSpec((T, LANES), lambda b: (b, 0)),
        compiler_params=_cparams(1),
        name="foxcum",
    )(proj, bias_row)


def _online_softmax_step_t(st, vt_t, m_ref, l_ref, acc_ref):
    m_old = m_ref[...]
    m_new = jnp.maximum(m_old, jnp.max(st, axis=0, keepdims=True))
    alpha = jnp.exp2(m_old - m_new)
    p = jnp.exp2(st - m_new)
    l_ref[...] = alpha * l_ref[...] + jnp.sum(p, axis=0, keepdims=True)
    acc_ref[...] = alpha * acc_ref[...] + _dot(vt_t, p.astype(MXU_DT))
    m_ref[...] = m_new


def _fox_body(q_ref, k_ref, v_ref, cum_ref, o_ref,
              ka_ref, vt_ref, m_ref, l_ref, acc_ref, sa_ref, sb_ref, *, T, TQ, NH):
    hp = pl.program_id(1)
    i = pl.program_id(2)
    RC = 512
    heads = range(NH)

    def cum_pieces(hh, rows, n):
        lane = lax.broadcasted_iota(jnp.int32, (n, LANES), 1)
        col = jnp.sum(jnp.where(lane == hp * NH + hh, cum_ref[rows, :], 0.0), axis=1, keepdims=True)
        return [p.astype(F32) for p in _split3(col * LOG2E)]

    @pl.when(i == 0)
    def _():
        for hh in heads:
            hcols = slice(hh * HEAD_DIM, (hh + 1) * HEAD_DIM)
            for c0 in range(0, T, RC):
                sl = slice(c0, c0 + RC)
                ka_ref[hh, sl, 0:HEAD_DIM] = k_ref[sl, hcols].astype(MXU_DT)
                lane = lax.broadcasted_iota(jnp.int32, (RC, LANES), 1)
                aug = jnp.where((lane >= 3) & (lane < 6), 1.0, 0.0)
                for idx, c in enumerate(cum_pieces(hh, sl, RC)):
                    aug = jnp.where(lane == idx, c, aug)
                ka_ref[hh, sl, HEAD_DIM:HEAD_DIM + LANES] = aug.astype(MXU_DT)
                vt_ref[hh, :, sl] = v_ref[sl, hcols].T.astype(MXU_DT)

    q0 = pl.multiple_of(i * TQ, TQ)
    lane = lax.broadcasted_iota(jnp.int32, (TQ, LANES), 1)
    qa = []
    for hh in heads:
        qaug = jnp.where(lane < 3, -1.0, 0.0)
        for idx, c in enumerate(cum_pieces(hh, pl.ds(q0, TQ), TQ)):
            qaug = jnp.where(lane == 3 + idx, c, qaug)
        qh = q_ref[:, hh * HEAD_DIM:(hh + 1) * HEAD_DIM] * QSCALE
        qa.append(jnp.concatenate([qh.astype(MXU_DT), qaug.astype(MXU_DT)], axis=1))
    m_ref[...] = jnp.full(m_ref.shape, NEG_INF, F32)
    l_ref[...] = jnp.zeros(l_ref.shape, F32)
    acc_ref[...] = jnp.zeros(acc_ref.shape, F32)

    def scores(t, s_ref):
        k0 = pl.multiple_of(t * TQ, TQ)
        for hh in heads:
            s_ref[hh] = _dot_nt(ka_ref[hh, pl.ds(k0, TQ), :], qa[hh])

    def consume(t, s_ref, diagonal):
        k0 = pl.multiple_of(t * TQ, TQ)
        for hh in heads:
            st = s_ref[hh]
            if diagonal:
                krow = lax.broadcasted_iota(jnp.int32, (TQ, TQ), 0)
                qcol = lax.broadcasted_iota(jnp.int32, (TQ, TQ), 1)
                st = jnp.where(krow <= qcol, st, NEG_INF)
            _online_softmax_step_t(st, vt_ref[hh, :, pl.ds(k0, TQ)],
                                   m_ref.at[hh], l_ref.at[hh], acc_ref.at[hh])

    scores(0, sa_ref)

    def pair(jj, carry):
        scores(2 * jj + 1, sb_ref)
        consume(2 * jj, sa_ref, False)
        scores(2 * jj + 2, sa_ref)
        consume(2 * jj + 1, sb_ref, False)
        return carry

    lax.fori_loop(0, i // 2, pair, 0)

    @pl.when(i % 2 == 0)
    def _():
        consume(i, sa_ref, True)

    @pl.when(i % 2 == 1)
    def _():
        scores(i, sb_ref)
        consume(i - 1, sa_ref, False)
        consume(i, sb_ref, True)

    for hh in heads:
        o_ref[:, hh * HEAD_DIM:(hh + 1) * HEAD_DIM] = (acc_ref[hh] / l_ref[hh]).T.astype(o_ref.dtype)


def _fox(proj, cum, B, T):
    TQ = 512
    NH = 2
    nq = T // TQ
    hw = NH * HEAD_DIM
    return pl.pallas_call(
        functools.partial(_fox_body, T=T, TQ=TQ, NH=NH),
        out_shape=jax.ShapeDtypeStruct((B * T, FOX_W), MXU_DT),
        grid=(B, FOX_HEADS // NH, nq),
        in_specs=[pl.BlockSpec((TQ, hw), lambda b, h, i: (b * nq + i, CB_FQ // NH + h)),
                  pl.BlockSpec((T, hw), lambda b, h, i: (b, CB_FK // NH + h)),
                  pl.BlockSpec((T, hw), lambda b, h, i: (b, CB_FV // NH + h)),
                  pl.BlockSpec((T, LANES), lambda b, h, i: (b, 0))],
        out_specs=pl.BlockSpec((TQ, hw), lambda b, h, i: (b * nq + i, h)),
        scratch_shapes=[pltpu.VMEM((NH, T, HEAD_DIM + LANES), MXU_DT),
                        pltpu.VMEM((NH, HEAD_DIM, T), MXU_DT),
                        pltpu.VMEM((NH, 1, TQ), F32),
                        pltpu.VMEM((NH, 1, TQ), F32),
                        pltpu.VMEM((NH, HEAD_DIM, TQ), F32),
                        pltpu.VMEM((NH, TQ, TQ), F32),
                        pltpu.VMEM((NH, TQ, TQ), F32)],
        compiler_params=_cparams(3),
        name="fox",
    )(proj, proj, proj, cum)


def _gelu_tanh(x):
    return 0.5 * x * (1.0 + jnp.tanh(np.sqrt(2.0 / np.pi).astype(np.float32)
                                     * (x + 0.044715 * (x * x * x))))


def _compress_one(x_ref, pe_ref, w1_ref, w2_ref, NCP):
    h0 = jnp.zeros((NCP, CMP_HIDDEN), F32)
    h1 = jnp.zeros((NCP, CMP_HIDDEN), F32)
    for l in range(CMP_STRIDE):
        xl = x_ref[pl.ds(l, NCP, stride=CMP_STRIDE), :]
        a0 = (xl + pe_ref[l:l + 1, :]).astype(MXU_DT)
        a1 = (xl + pe_ref[CMP_STRIDE + l:CMP_STRIDE + l + 1, :]).astype(MXU_DT)
        h0 = h0 + _dot(a0, w1_ref[l * HEAD_DIM:(l + 1) * HEAD_DIM, :])
        h1 = h1 + _dot(a1, w1_ref[(CMP_STRIDE + l) * HEAD_DIM:(CMP_STRIDE + l + 1) * HEAD_DIM, :])
    hsum = h0 + pltpu.roll(h1, NCP - 1, 0)
    return _dot(_gelu_tanh(hsum).astype(MXU_DT), w2_ref[...])


def _compress_body(xk_ref, xv_ref, pek_ref, w1k_ref, w2k_ref, pev_ref, w1v_ref, w2v_ref,
                   cc_ref, sc_ref, kc_ref, vc_ref, *, NCP):
    kc = _compress_one(xk_ref, pek_ref, w1k_ref, w2k_ref, NCP)
    kc_ref[0, 0] = _rope(kc, cc_ref[...], sc_ref[...]).astype(kc_ref.dtype)
    vc = _compress_one(xv_ref, pev_ref, w1v_ref, w2v_ref, NCP)
    vc_ref[0, 0] = vc.T.astype(vc_ref.dtype)


def _compress(proj, pek, w1k, w2k, pev, w1v, w2v, cc, sc, B, T):
    NCP = T // CMP_STRIDE
    full = lambda shape: pl.BlockSpec(shape, lambda b, g: (0,) * len(shape))
    k_sds = jax.ShapeDtypeStruct((B, NSA_KV_GROUPS, NCP, HEAD_DIM), MXU_DT)
    v_sds = jax.ShapeDtypeStruct((B, NSA_KV_GROUPS, HEAD_DIM, NCP), MXU_DT)
    k_spec = pl.BlockSpec((1, 1, NCP, HEAD_DIM), lambda b, g: (b, g, 0, 0))
    v_spec = pl.BlockSpec((1, 1, HEAD_DIM, NCP), lambda b, g: (b, g, 0, 0))
    return pl.pallas_call(
        functools.partial(_compress_body, NCP=NCP),
        out_shape=(k_sds, v_sds),
        grid=(B, NSA_KV_GROUPS),
        in_specs=[pl.BlockSpec((T, LANES), lambda b, g: (b, CB_NKC + g)),
                  pl.BlockSpec((T, LANES), lambda b, g: (b, CB_NVC + g)),
                  full((CMP_LEN, HEAD_DIM)), full((CMP_LEN * HEAD_DIM, CMP_HIDDEN)),
                  full((CMP_HIDDEN, HEAD_DIM)),
                  full((CMP_LEN, HEAD_DIM)), full((CMP_LEN * HEAD_DIM, CMP_HIDDEN)),
                  full((CMP_HIDDEN, HEAD_DIM)),
                  full((NCP, LANES)), full((NCP, LANES))],
        out_specs=(k_spec, v_spec),
        compiler_params=_cparams(2),
        name="compress",
    )(proj, proj, pek, w1k, w2k, pev, w1v, w2v, cc, sc)


def _nsa_body(q_ref, sm_ref, cq_ref, sq_ref, kc_ref, vc_ref, ks_ref, vs_ref, kw_ref, vw_ref,
              ct_ref, st_ref, ovt_ref, o_ref,
              ksa_ref, vst_ref, kwb_ref, vwt_ref, smt_ref, m_ref, l_ref, acc_ref, sa_ref, sb_ref,
              part_ref, *, T, TQ, TK, WSP, NB, NCP, NSEL):
    g = pl.program_id(1)
    i = pl.program_id(2)
    HG = NSA_GROUP_SIZE
    R = HG * TQ
    RC = 512

    @pl.when(i == 0)
    def _():
        for c0 in range(0, T, RC):
            sl = slice(c0, c0 + RC)
            ct = ct_ref[sl, :]
            st = st_ref[sl, :]
            ksa_ref[sl, 0:HEAD_DIM] = _rope(ks_ref[sl, :], ct, st).astype(MXU_DT)
            kblk = (c0 + lax.broadcasted_iota(jnp.int32, (RC, LANES), 0)) >> 6
            onehot = jnp.where(kblk == lax.broadcasted_iota(jnp.int32, (RC, LANES), 1), 1.0, 0.0)
            ksa_ref[sl, HEAD_DIM:HEAD_DIM + LANES] = onehot.astype(MXU_DT)
            kwb_ref[sl, :] = _rope(kw_ref[sl, :], ct, st).astype(MXU_DT)
            vst_ref[:, sl] = vs_ref[sl, :].T.astype(MXU_DT)
            vwt_ref[:, sl] = vw_ref[sl, :].T.astype(MXU_DT)

    q0 = i * TQ
    cq = cq_ref[...]
    sq = sq_ref[...]
    qb = jnp.concatenate(
        [_rope(q_ref[:, hh * HEAD_DIM:(hh + 1) * HEAD_DIM], cq, sq) * QSCALE for hh in range(HG)],
        axis=0).astype(MXU_DT)
    qpos = q0 + (lax.broadcasted_iota(jnp.int32, (1, R), 1) & (TQ - 1))

    sc = _dot_nt(kc_ref[0, 0], qb)
    nrow = lax.broadcasted_iota(jnp.int32, (NCP, R), 0)
    last_valid = (qpos - (CMP_LEN - 1)) >> 4
    sc = jnp.where(nrow <= last_valid, sc, NEG_INF)
    pc = jnp.exp2(sc - jnp.max(sc, axis=0, keepdims=True))
    pc = pc * (jnp.where(qpos >= CMP_LEN - 1, 1.0, 0.0) / jnp.sum(pc, axis=0, keepdims=True))
    o_c = _dot(vc_ref[0, 0], pc.astype(MXU_DT))

    smt_ref[...] = sm_ref[...].T

    def gate(hh, c):
        return _sigmoid(smt_ref[pl.ds(SM_GATE + 3 * (g * HG + hh) + c, 1), :])

    w0 = pl.multiple_of(jnp.clip((i + 1) * TQ - WSP, 0, T - WSP), LANES)
    kt = kwb_ref[pl.ds(w0, WSP), :]
    vt_t = vwt_ref[:, pl.ds(w0, WSP)]
    dist = (q0 + lax.broadcasted_iota(jnp.int32, (WSP, TQ), 1)) \
        - (w0 + lax.broadcasted_iota(jnp.int32, (WSP, TQ), 0))
    wbias = jnp.where((dist >= 0) & (dist < WINDOW), 0.0, NEG_INF)
    sw = _dot_nt(kt, qb) + pltpu.repeat(wbias, HG, axis=1)
    pw = jnp.exp2(sw - jnp.max(sw, axis=0, keepdims=True))
    o_w = _dot(vt_t, pw.astype(MXU_DT)) / jnp.sum(pw, axis=0, keepdims=True)
    gate_c = jnp.concatenate([gate(hh, 0) for hh in range(HG)], axis=1)
    gate_w = jnp.concatenate([gate(hh, 2) for hh in range(HG)], axis=1)
    part_ref[...] = gate_c * o_c + gate_w * o_w

    psum = (pc[:, 0:TQ] + pc[:, TQ:2 * TQ]) + (pc[:, 2 * TQ:3 * TQ] + pc[:, 3 * TQ:4 * TQ])
    hi, mid, lo = _split3(psum)
    ovt = ovt_ref[...]
    imp = (_dot(ovt, hi) + _dot(ovt, mid)) + _dot(ovt, lo)
    jblk = lax.broadcasted_iota(jnp.int32, (NB, TQ), 0)
    cur = (q0 + lax.broadcasted_iota(jnp.int32, (NB, TQ), 1)) >> 6
    forced = (jblk == 0) | (jblk == cur) | (jblk == cur - 1)
    imp = jnp.where(forced, FORCE_SCORE, jnp.where(jblk > cur, -1.0, imp))
    SUB = 8
    groups = [imp[r * SUB:(r + 1) * SUB, :] for r in range(NB // SUB)]
    sub = lax.broadcasted_iota(jnp.int32, (SUB, TQ), 0)
    ranks = [jnp.zeros((SUB, TQ), F32) for _ in groups]
    for jp in range(NB):
        gj, sj = divmod(jp, SUB)
        row = groups[gj][sj:sj + 1, :]
        for r, grp in enumerate(groups):
            if r > gj:
                beats = jnp.where(row >= grp, 1.0, 0.0)
            elif r < gj:
                beats = jnp.where(row > grp, 1.0, 0.0)
            else:
                beats = jnp.where(sub > sj, jnp.where(row >= grp, 1.0, 0.0), jnp.where(row > grp, 1.0, 0.0))
            ranks[r] = ranks[r] + beats
    rank = jnp.concatenate(ranks, axis=0)
    sel_t = jnp.where((rank < NSEL) & (imp >= 0.0), 1.0, 0.0)
    if NB < LANES:
        sel_t = jnp.concatenate([sel_t, jnp.zeros((LANES - NB, TQ), F32)], axis=0)
    selneg = jnp.where(sel_t.T > 0.5, 0.0, NEG_INF).astype(MXU_DT)
    qa = jnp.concatenate([qb, jnp.concatenate([selneg] * HG, axis=0)], axis=1)

    m_ref[...] = jnp.full(m_ref.shape, NEG_INF, F32)
    l_ref[...] = jnp.zeros(l_ref.shape, F32)
    acc_ref[...] = jnp.zeros(acc_ref.shape, F32)

    def scores(t, s_ref):
        s_ref[...] = _dot_nt(ksa_ref[pl.ds(pl.multiple_of(t * TK, TK), TK), :], qa)

    def consume(t, s_ref, causal):
        k0 = pl.multiple_of(t * TK, TK)
        if causal:
            own = pl.ds(pl.multiple_of(q0 - k0, TQ), TQ)
            krow = lax.broadcasted_iota(jnp.int32, (TQ, TQ), 0)
            qcol = lax.broadcasted_iota(jnp.int32, (TQ, TQ), 1)
            tri = jnp.where(krow <= qcol, 0.0, NEG_INF)
            s_ref[own, :] = s_ref[own, :] + pltpu.repeat(tri, HG, axis=1)
        _online_softmax_step_t(s_ref[...], vst_ref[:, pl.ds(k0, TK)], m_ref, l_ref, acc_ref)

    last = ((i + 1) * TQ + TK - 1) // TK - 1
    scores(0, sa_ref)

    def pair(jj, carry):
        scores(2 * jj + 1, sb_ref)
        consume(2 * jj, sa_ref, False)
        scores(2 * jj + 2, sa_ref)
        consume(2 * jj + 1, sb_ref, False)
        return carry

    lax.fori_loop(0, last // 2, pair, 0)

    @pl.when(last % 2 == 0)
    def _():
        consume(last, sa_ref, True)

    @pl.when(last % 2 == 1)
    def _():
        scores(last, sb_ref)
        consume(last - 1, sa_ref, False)
        consume(last, sb_ref, True)

    gate_s = jnp.concatenate([gate(hh, 1) for hh in range(HG)], axis=1)
    o = part_ref[...] + (gate_s / l_ref[...]) * acc_ref[...]
    for hh in range(HG):
        o_ref[:, hh * HEAD_DIM:(hh + 1) * HEAD_DIM] = o[:, hh * TQ:(hh + 1) * TQ].T.astype(o_ref.dtype)


def _nsa(proj, kc, vc, ct, st, ovt, B, T):
    TQ = 128
    TK = 512
    WSP = WINDOW + TQ
    NB = T // SLC_LEN
    NCP = T // CMP_STRIDE
    nq = T // TQ
    HG = NSA_GROUP_SIZE
    qw = HG * HEAD_DIM
    kv_spec = lambda cb: pl.BlockSpec((T, LANES), lambda b, g, i: (b, cb + g))
    return pl.pallas_call(
        functools.partial(_nsa_body, T=T, TQ=TQ, TK=TK, WSP=WSP, NB=NB, NCP=NCP,
                          NSEL=min(SLC_TOP, NB)),
        out_shape=jax.ShapeDtypeStruct((B * T, NSA_W), MXU_DT),
        grid=(B, NSA_KV_GROUPS, nq),
        in_specs=[pl.BlockSpec((TQ, qw), lambda b, g, i: (b * nq + i, CB_NQ * LANES // qw + g)),
                  pl.BlockSpec((TQ, LANES), lambda b, g, i: (b * nq + i, CB_SM)),
                  pl.BlockSpec((TQ, LANES), lambda b, g, i: (i, 0)),
                  pl.BlockSpec((TQ, LANES), lambda b, g, i: (i, 0)),
                  pl.BlockSpec((1, 1, NCP, HEAD_DIM), lambda b, g, i: (b, g, 0, 0)),
                  pl.BlockSpec((1, 1, HEAD_DIM, NCP), lambda b, g, i: (b, g, 0, 0)),
                  kv_spec(CB_NKS), kv_spec(CB_NVS), kv_spec(CB_NKW), kv_spec(CB_NVW),
                  pl.BlockSpec((T, LANES), lambda b, g, i: (0, 0)),
                  pl.BlockSpec((T, LANES), lambda b, g, i: (0, 0)),
                  pl.BlockSpec((NB, NCP), lambda b, g, i: (0, 0))],
        out_specs=pl.BlockSpec((TQ, qw), lambda b, g, i: (b * nq + i, g)),
        scratch_shapes=[pltpu.VMEM((T, HEAD_DIM + LANES), MXU_DT),
                        pltpu.VMEM((HEAD_DIM, T), MXU_DT),
                        pltpu.VMEM((T, LANES), MXU_DT),
                        pltpu.VMEM((HEAD_DIM, T), MXU_DT),
                        pltpu.VMEM((LANES, TQ), F32),
                        pltpu.VMEM((1, HG * TQ), F32),
                        pltpu.VMEM((1, HG * TQ), F32),
                        pltpu.VMEM((HEAD_DIM, HG * TQ), F32),
                        pltpu.VMEM((TK, HG * TQ), F32),
                        pltpu.VMEM((TK, HG * TQ), F32),
                        pltpu.VMEM((HEAD_DIM, HG * TQ), F32)],
        compiler_params=_cparams(3),
        name="nsa",
    )(proj, proj, ct, st, kc, vc, proj, proj, proj, proj, ct, st, ovt)


def _merge_body(of_ref, on_ref, wf_ref, wn_ref, gf_ref, gn_ref, o_ref):
    yf = _dot(of_ref[...], wf_ref[...])
    yn = _dot(on_ref[...], wn_ref[...])
    o_ref[...] = (_sigmoid(gf_ref[...]) * yf + _sigmoid(gn_ref[...]) * yn).astype(o_ref.dtype)


def _merge(o_fox, o_nsa, wf, wn, layer, proj):
    m = o_fox.shape[0]
    tm = min(1024, m)
    tn = 512
    gfb = CB_GF * LANES // tn
    gnb = CB_GN * LANES // tn
    return pl.pallas_call(
        _merge_body,
        out_shape=jax.ShapeDtypeStruct((m, D_MODEL), MXU_DT),
        grid=(m // tm, D_MODEL // tn),
        in_specs=[pl.BlockSpec((tm, FOX_W), lambda i, j: (i, 0)),
                  pl.BlockSpec((tm, NSA_W), lambda i, j: (i, 0)),
                  pl.BlockSpec((None, FOX_W, tn), lambda i, j: (layer, 0, j)),
                  pl.BlockSpec((None, NSA_W, tn), lambda i, j: (layer, 0, j)),
                  pl.BlockSpec((tm, tn), lambda i, j: (i, gfb + j)),
                  pl.BlockSpec((tm, tn), lambda i, j: (i, gnb + j))],
        out_specs=pl.BlockSpec((tm, tn), lambda i, j: (i, j)),
        compiler_params=_cparams(2),
        name="merge",
    )(o_fox, o_nsa, wf, wn, proj, proj)


def _outproj_body(a_ref, w_ref, g_ref, x_ref, o_ref):
    y = _dot(a_ref[...], w_ref[...])
    o_ref[...] = x_ref[...] + _rms(y) * g_ref[...]


def _outproj(mix, w, layer, g, x2):
    m, d = x2.shape
    tm = min(512, m)
    return pl.pallas_call(
        _outproj_body,
        out_shape=jax.ShapeDtypeStruct((m, d), F32),
        grid=(m // tm,),
        in_specs=[pl.BlockSpec((tm, d), lambda i: (i, 0)),
                  pl.BlockSpec((None, d, d), lambda i: (layer, 0, 0)),
                  pl.BlockSpec((1, d), lambda i: (0, 0)),
                  pl.BlockSpec((tm, d), lambda i: (i, 0))],
        out_specs=pl.BlockSpec((tm, d), lambda i: (i, 0)),
        compiler_params=_cparams(1),
        name="outproj",
    )(mix, w, g, x2)


def _ffn_up_body(x_ref, g_ref, wg_ref, wu_ref, o_ref, a_ref):
    @pl.when(pl.program_id(1) == 0)
    def _():
        a_ref[...] = (_rms(x_ref[...]) * g_ref[...]).astype(a_ref.dtype)

    a = a_ref[...]
    hg = _dot(a, wg_ref[...])
    hu = _dot(a, wu_ref[...])
    o_ref[...] = (hg * _sigmoid(hg) * hu).astype(o_ref.dtype)


def _ffn_up(x2, g, wg, wu, layer):
    m, d = x2.shape
    f = wg.shape[2]
    tm = min(1024, m)
    tn = 512
    return pl.pallas_call(
        _ffn_up_body,
        out_shape=jax.ShapeDtypeStruct((m, f), MXU_DT),
        grid=(m // tm, f // tn),
        in_specs=[pl.BlockSpec((tm, d), lambda i, j: (i, 0)),
                  pl.BlockSpec((1, d), lambda i, j: (0, 0)),
                  pl.BlockSpec((None, d, tn), lambda i, j: (layer, 0, j)),
                  pl.BlockSpec((None, d, tn), lambda i, j: (layer, 0, j))],
        out_specs=pl.BlockSpec((tm, tn), lambda i, j: (i, j)),
        scratch_shapes=[pltpu.VMEM((tm, d), MXU_DT)],
        compiler_params=_cparams(2),
        name="ffn_up",
    )(x2, g, wg, wu)


def _ffn_down_body(h_ref, w_ref, g_ref, x_ref, o_ref):
    y = _dot(h_ref[...], w_ref[...])
    o_ref[...] = x_ref[...] + _rms(y) * g_ref[...]


def _ffn_down(h, w, layer, g, x2):
    m, d = x2.shape
    f = h.shape[1]
    tm = min(256, m)
    return pl.pallas_call(
        _ffn_down_body,
        out_shape=jax.ShapeDtypeStruct((m, d), F32),
        grid=(m // tm,),
        in_specs=[pl.BlockSpec((tm, f), lambda i: (i, 0)),
                  pl.BlockSpec((None, f, d), lambda i: (layer, 0, 0), pipeline_mode=pl.Buffered(1)),
                  pl.BlockSpec((1, d), lambda i: (0, 0)),
                  pl.BlockSpec((tm, d), lambda i: (i, 0))],
        out_specs=pl.BlockSpec((tm, d), lambda i: (i, 0)),
        compiler_params=_cparams(1),
        name="ffn_down",
    )(h, w, g, x2)


def _pack_w_in(w):
    offs = np.cumsum([0] + IN_SIZES)
    seg = lambda k: w[:, offs[k]:offs[k + 1]]
    small = jnp.concatenate(
        [seg(3), seg(11), jnp.zeros((w.shape[0], LANES - FOX_HEADS - 3 * NSA_HEADS), w.dtype)], axis=1)
    cols = [w[:, offs[0]:offs[3]], w[:, offs[4]:offs[11]], w[:, offs[12]:offs[14]], small]
    return jnp.concatenate(cols, axis=1).astype(MXU_DT)


def _rope_tables(pos):
    inv_freq = jnp.power(ROPE_THETA, -jnp.arange(ROPE_HALF, dtype=F32) * (2.0 / ROPE_DIM))
    ang = pos[:, None] * inv_freq[None, :]
    cos, sin = jnp.cos(ang), jnp.sin(ang)
    n = pos.shape[0]
    pad = HEAD_DIM - ROPE_DIM
    c = jnp.concatenate([cos, cos, jnp.ones((n, pad), F32)], axis=1)
    s = jnp.concatenate([-sin, sin, jnp.zeros((n, pad), F32)], axis=1)
    return c, s


def _overlap_t(T):
    nb, ncp = T // SLC_LEN, T // CMP_STRIDE
    sc = np.arange(ncp) * CMP_STRIDE
    ss = np.arange(nb) * SLC_LEN
    ov = np.minimum(sc[None, :] + CMP_LEN, ss[:, None] + SLC_LEN) - np.maximum(sc[None, :], ss[:, None])
    ov = np.clip(ov, 0, None) / CMP_LEN
    ov[:, ncp - 1] = 0.0
    return jnp.asarray(ov, dtype=MXU_DT)


def _layer(x2, B, T, layer, n_mix_pre, n_mix_post, n_ffn_pre, n_ffn_post, w_in, f_bias,
           ck_pe, ck_w1, ck_w2, cv_pe, cv_w1, cv_w2, stacked, tables):
    ct, st, cc, sc, ovt = tables
    w_up_fox, w_up_nsa, w_out, w_gate, w_up, w_down = stacked
    row = lambda v: v.reshape(1, -1).astype(F32)
    bf = lambda w: w.astype(MXU_DT)
    proj = _inproj(x2, row(n_mix_pre), _pack_w_in(w_in))
    bias_row = jnp.pad(f_bias.astype(F32), (0, LANES - FOX_HEADS)).reshape(1, LANES)
    cum = _foxcum(proj, bias_row, B, T)
    o_fox = _fox(proj, cum, B, T)
    kc, vc = _compress(proj, ck_pe, bf(ck_w1), bf(ck_w2), cv_pe, bf(cv_w1), bf(cv_w2), cc, sc, B, T)
    o_nsa = _nsa(proj, kc, vc, ct, st, ovt, B, T)
    mix = _merge(o_fox, o_nsa, w_up_fox, w_up_nsa, layer, proj)
    x2 = _outproj(mix, w_out, layer, row(n_mix_post), x2)
    h = _ffn_up(x2, row(n_ffn_pre), w_gate, w_up, layer)
    return _ffn_down(h, w_down, layer, row(n_ffn_post), x2)


@jax.jit
def kernel(x, norm_mix_pre, norm_mix_post, norm_ffn_pre, norm_ffn_post, w_in, fox_forget_bias, cmp_k_pe, cmp_k_w1, cmp_k_w2, cmp_v_pe, cmp_v_w1, cmp_v_w2, w_up_fox, w_up_nsa, w_out, w_ffn_gate, w_ffn_up, w_ffn_down):
    B, T, D = x.shape
    ct, st = _rope_tables(jnp.arange(T, dtype=F32))
    ncp = T // CMP_STRIDE
    cc, sc = _rope_tables((jnp.arange(ncp) * CMP_STRIDE + CMP_LEN - 1).astype(F32))
    tables = (ct, st, cc, sc, _overlap_t(T))
    x2 = x.reshape(B * T, D)
    stacked = tuple(w.astype(MXU_DT)
                    for w in (w_up_fox, w_up_nsa, w_out, w_ffn_gate, w_ffn_up, w_ffn_down))
    for l in range(w_in.shape[0]):
        x2 = _layer(x2, B, T, l, norm_mix_pre[l], norm_mix_post[l], norm_ffn_pre[l], norm_ffn_post[l],
                    w_in[l], fox_forget_bias[l],
                    cmp_k_pe[l], cmp_k_w1[l], cmp_k_w2[l], cmp_v_pe[l], cmp_v_w1[l], cmp_v_w2[l],
                    stacked, tables)
    return x2.reshape(B, T, D)
```

```python
import functools

import numpy as np
import jax
import jax.numpy as jnp
from jax import lax
from jax.experimental import pallas as pl
from jax.experimental.pallas import tpu as pltpu

D_MODEL = 2048
HEAD_DIM = 128
FOX_HEADS = 8
NSA_HEADS = 8
NSA_GROUP_SIZE = 4
NSA_KV_GROUPS = 2
FOX_W = FOX_HEADS * HEAD_DIM
NSA_W = NSA_HEADS * HEAD_DIM
NSA_KV_W = NSA_KV_GROUPS * HEAD_DIM
ROPE_DIM = HEAD_DIM // 4
ROPE_HALF = ROPE_DIM // 2
ROPE_THETA = 500000.0
CMP_LEN = 32
CMP_STRIDE = 16
CMP_HIDDEN = 256
SLC_LEN = 64
SLC_TOP = 16
WINDOW = 512
FFN_HIDDEN = 5632
EPS = 1e-6
NEG_INF = -1e30
FORCE_SCORE = 1e6
SCALE = HEAD_DIM ** -0.5
LOG2E = float(np.log2(np.e))
QSCALE = SCALE * LOG2E
IN_SIZES = [FOX_W, FOX_W, FOX_W, FOX_HEADS, NSA_W,
            NSA_KV_W, NSA_KV_W, NSA_KV_W, NSA_KV_W, NSA_KV_W, NSA_KV_W,
            NSA_HEADS * 3, D_MODEL, D_MODEL]

LANES = 128
MXU_DT = jnp.bfloat16
VMEM_LIMIT = 56 * 1024 * 1024

CB_FQ, CB_FK, CB_FV, CB_NQ = 0, 8, 16, 24
CB_NKC, CB_NVC, CB_NKS, CB_NVS, CB_NKW, CB_NVW = 32, 34, 36, 38, 40, 42
CB_GF, CB_GN, CB_SM = 44, 60, 76
NP_COLS = 77 * LANES
SM_FF, SM_GATE = 0, FOX_HEADS

F32 = jnp.float32


def _cparams(n_axes):
    return pltpu.CompilerParams(dimension_semantics=("arbitrary",) * n_axes,
                                vmem_limit_bytes=VMEM_LIMIT)


def _rms(x):
    return x * lax.rsqrt(jnp.mean(x * x, axis=-1, keepdims=True) + EPS)


def _dot(a, b):
    return jnp.dot(a, b, preferred_element_type=F32)


def _dot_nt(a, b):
    return lax.dot_general(a, b, (((1,), (1,)), ((), ())), preferred_element_type=F32)


def _split3(x):
    hi = x.astype(MXU_DT)
    r1 = x - hi.astype(F32)
    mid = r1.astype(MXU_DT)
    lo = (r1 - mid.astype(F32)).astype(MXU_DT)
    return hi, mid, lo


def _rope(x, c, s):
    lane = lax.broadcasted_iota(jnp.int32, x.shape, 1)
    partner = jnp.where(lane < ROPE_HALF,
                        pltpu.roll(x, LANES - ROPE_HALF, 1),
                        pltpu.roll(x, ROPE_HALF, 1))
    return x * c + partner * s


def _sigmoid(z):
    return 1.0 / (1.0 + jnp.exp(-z))


def _inproj_body(x_ref, g_ref, w_ref, o_ref, a_ref):
    @pl.when(pl.program_id(1) == 0)
    def _():
        a_ref[...] = (_rms(x_ref[...]) * g_ref[...]).astype(a_ref.dtype)

    o_ref[...] = _dot(a_ref[...], w_ref[...])


def _inproj(x2, g, w):
    m, d = x2.shape
    n = w.shape[1]
    tm = min(1024, m)
    tn = 11 * LANES
    return pl.pallas_call(
        _inproj_body,
        out_shape=jax.ShapeDtypeStruct((m, n), F32),
        grid=(m // tm, n // tn),
        in_specs=[pl.BlockSpec((tm, d), lambda i, j: (i, 0)),
                  pl.BlockSpec((1, d), lambda i, j: (0, 0)),
                  pl.BlockSpec((d, tn), lambda i, j: (0, j))],
        out_specs=pl.BlockSpec((tm, tn), lambda i, j: (i, j)),
        scratch_shapes=[pltpu.VMEM((tm, d), MXU_DT)],
        compiler_params=_cparams(2),
        name="inproj",
    )(x2, g, w)


def _foxcum_body(s_ref, b_ref, cum_ref, *, T, CH):
    r = lax.broadcasted_iota(jnp.int32, (CH, CH), 0)
    c = lax.broadcasted_iota(jnp.int32, (CH, CH), 1)
    tri = jnp.where(c <= r, 1.0, 0.0).astype(MXU_DT)
    carry = jnp.zeros((1, LANES), F32)
    for ci in range(T // CH):
        z = s_ref[ci * CH:(ci + 1) * CH, :] + b_ref[...]
        lf = jnp.minimum(z, 0.0) - jnp.log1p(jnp.exp(-jnp.abs(z)))
        hi, mid, lo = _split3(lf)
        out = (_dot(tri, hi) + _dot(tri, mid)) + _dot(tri, lo) + carry
        cum_ref[ci * CH:(ci + 1) * CH, :] = out
        carry = out[CH - 1:CH, :]


def _foxcum(proj, bias_row, B, T):
    CH = 256
    return pl.pallas_call(
        functools.partial(_foxcum_body, T=T, CH=CH),
        out_shape=jax.ShapeDtypeStruct((B * T, LANES), F32),
        grid=(B,),
        in_specs=[pl.BlockSpec((T, LANES), lambda b: (b, CB_SM)),
                  pl.BlockSpec((1, LANES), lambda b: (0, 0))],
        out_specs=pl.BlockSpec((T, LANES), lambda b: (b, 0)),
        compiler_params=_cparams(1),
        name="foxcum",
    )(proj, bias_row)


def _online_softmax_step_t(st, vt_t, m_ref, l_ref, acc_ref):
    m_old = m_ref[...]
    m_new = jnp.maximum(m_old, jnp.max(st, axis=0, keepdims=True))
    alpha = jnp.exp2(m_old - m_new)
    p = jnp.exp2(st - m_new)
    l_ref[...] = alpha * l_ref[...] + jnp.sum(p, axis=0, keepdims=True)
    acc_ref[...] = alpha * acc_ref[...] + _dot(vt_t, p.astype(MXU_DT))
    m_ref[...] = m_new


def _fox_body(q_ref, k_ref, v_ref, cum_ref, o_ref,
              ka_ref, vt_ref, m_ref, l_ref, acc_ref, sa_ref, sb_ref, *, T, TQ, NH):
    hp = pl.program_id(1)
    i = pl.program_id(2)
    RC = 512
    heads = range(NH)

    def cum_pieces(hh, rows, n):
        lane = lax.broadcasted_iota(jnp.int32, (n, LANES), 1)
        col = jnp.sum(jnp.where(lane == hp * NH + hh, cum_ref[rows, :], 0.0), axis=1, keepdims=True)
        return [p.astype(F32) for p in _split3(col * LOG2E)]

    @pl.when(i == 0)
    def _():
        for hh in heads:
            hcols = slice(hh * HEAD_DIM, (hh + 1) * HEAD_DIM)
            for c0 in range(0, T, RC):
                sl = slice(c0, c0 + RC)
                ka_ref[hh, sl, 0:HEAD_DIM] = k_ref[sl, hcols].astype(MXU_DT)
                lane = lax.broadcasted_iota(jnp.int32, (RC, LANES), 1)
                aug = jnp.where((lane >= 3) & (lane < 6), 1.0, 0.0)
                for idx, c in enumerate(cum_pieces(hh, sl, RC)):
                    aug = jnp.where(lane == idx, c, aug)
                ka_ref[hh, sl, HEAD_DIM:HEAD_DIM + LANES] = aug.astype(MXU_DT)
                vt_ref[hh, :, sl] = v_ref[sl, hcols].T.astype(MXU_DT)

    q0 = pl.multiple_of(i * TQ, TQ)
    lane = lax.broadcasted_iota(jnp.int32, (TQ, LANES), 1)
    qa = []
    for hh in heads:
        qaug = jnp.where(lane < 3, -1.0, 0.0)
        for idx, c in enumerate(cum_pieces(hh, pl.ds(q0, TQ), TQ)):
            qaug = jnp.where(lane == 3 + idx, c, qaug)
        qh = q_ref[:, hh * HEAD_DIM:(hh + 1) * HEAD_DIM] * QSCALE
        qa.append(jnp.concatenate([qh.astype(MXU_DT), qaug.astype(MXU_DT)], axis=1))
    m_ref[...] = jnp.full(m_ref.shape, NEG_INF, F32)
    l_ref[...] = jnp.zeros(l_ref.shape, F32)
    acc_ref[...] = jnp.zeros(acc_ref.shape, F32)

    def scores(t, s_ref):
        k0 = pl.multiple_of(t * TQ, TQ)
        for hh in heads:
            s_ref[hh] = _dot_nt(ka_ref[hh, pl.ds(k0, TQ), :], qa[hh])

    def consume(t, s_ref, diagonal):
        k0 = pl.multiple_of(t * TQ, TQ)
        for hh in heads:
            st = s_ref[hh]
            if diagonal:
                krow = lax.broadcasted_iota(jnp.int32, (TQ, TQ), 0)
                qcol = lax.broadcasted_iota(jnp.int32, (TQ, TQ), 1)
                st = jnp.where(krow <= qcol, st, NEG_INF)
            _online_softmax_step_t(st, vt_ref[hh, :, pl.ds(k0, TQ)],
                                   m_ref.at[hh], l_ref.at[hh], acc_ref.at[hh])

    scores(0, sa_ref)

    def pair(jj, carry):
        scores(2 * jj + 1, sb_ref)
        consume(2 * jj, sa_ref, False)
        scores(2 * jj + 2, sa_ref)
        consume(2 * jj + 1, sb_ref, False)
        return carry

    lax.fori_loop(0, i // 2, pair, 0)

    @pl.when(i % 2 == 0)
    def _():
        consume(i, sa_ref, True)

    @pl.when(i % 2 == 1)
    def _():
        scores(i, sb_ref)
        consume(i - 1, sa_ref, False)
        consume(i, sb_ref, True)

    for hh in heads:
        o_ref[:, hh * HEAD_DIM:(hh + 1) * HEAD_DIM] = (acc_ref[hh] / l_ref[hh]).T.astype(o_ref.dtype)


def _fox(proj, cum, B, T):
    TQ = 512
    NH = 2
    nq = T // TQ
    hw = NH * HEAD_DIM
    return pl.pallas_call(
        functools.partial(_fox_body, T=T, TQ=TQ, NH=NH),
        out_shape=jax.ShapeDtypeStruct((B * T, FOX_W), MXU_DT),
        grid=(B, FOX_HEADS // NH, nq),
        in_specs=[pl.BlockSpec((TQ, hw), lambda b, h, i: (b * nq + i, CB_FQ // NH + h)),
                  pl.BlockSpec((T, hw), lambda b, h, i: (b, CB_FK // NH + h)),
                  pl.BlockSpec((T, hw), lambda b, h, i: (b, CB_FV // NH + h)),
                  pl.BlockSpec((T, LANES), lambda b, h, i: (b, 0))],
        out_specs=pl.BlockSpec((TQ, hw), lambda b, h, i: (b * nq + i, h)),
        scratch_shapes=[pltpu.VMEM((NH, T, HEAD_DIM + LANES), MXU_DT),
                        pltpu.VMEM((NH, HEAD_DIM, T), MXU_DT),
                        pltpu.VMEM((NH, 1, TQ), F32),
                        pltpu.VMEM((NH, 1, TQ), F32),
                        pltpu.VMEM((NH, HEAD_DIM, TQ), F32),
                        pltpu.VMEM((NH, TQ, TQ), F32),
                        pltpu.VMEM((NH, TQ, TQ), F32)],
        compiler_params=_cparams(3),
        name="fox",
    )(proj, proj, proj, cum)


def _gelu_tanh(x):
    return 0.5 * x * (1.0 + jnp.tanh(np.sqrt(2.0 / np.pi).astype(np.float32)
                                     * (x + 0.044715 * (x * x * x))))


def _compress_one(x_ref, pe_ref, w1_ref, w2_ref, NCP):
    h0 = jnp.zeros((NCP, CMP_HIDDEN), F32)
    h1 = jnp.zeros((NCP, CMP_HIDDEN), F32)
    for l in range(CMP_STRIDE):
        xl = x_ref[pl.ds(l, NCP, stride=CMP_STRIDE), :]
        a0 = (xl + pe_ref[l:l + 1, :]).astype(MXU_DT)
        a1 = (xl + pe_ref[CMP_STRIDE + l:CMP_STRIDE + l + 1, :]).astype(MXU_DT)
        h0 = h0 + _dot(a0, w1_ref[l * HEAD_DIM:(l + 1) * HEAD_DIM, :])
        h1 = h1 + _dot(a1, w1_ref[(CMP_STRIDE + l) * HEAD_DIM:(CMP_STRIDE + l + 1) * HEAD_DIM, :])
    hsum = h0 + pltpu.roll(h1, NCP - 1, 0)
    return _dot(_gelu_tanh(hsum).astype(MXU_DT), w2_ref[...])


def _compress_body(xk_ref, xv_ref, pek_ref, w1k_ref, w2k_ref, pev_ref, w1v_ref, w2v_ref,
                   cc_ref, sc_ref, kc_ref, vc_ref, *, NCP):
    kc = _compress_one(xk_ref, pek_ref, w1k_ref, w2k_ref, NCP)
    kc_ref[0, 0] = _rope(kc, cc_ref[...], sc_ref[...]).astype(kc_ref.dtype)
    vc = _compress_one(xv_ref, pev_ref, w1v_ref, w2v_ref, NCP)
    vc_ref[0, 0] = vc.T.astype(vc_ref.dtype)


def _compress(proj, pek, w1k, w2k, pev, w1v, w2v, cc, sc, B, T):
    NCP = T // CMP_STRIDE
    full = lambda shape: pl.BlockSpec(shape, lambda b, g: (0,) * len(shape))
    k_sds = jax.ShapeDtypeStruct((B, NSA_KV_GROUPS, NCP, HEAD_DIM), MXU_DT)
    v_sds = jax.ShapeDtypeStruct((B, NSA_KV_GROUPS, HEAD_DIM, NCP), MXU_DT)
    k_spec = pl.BlockSpec((1, 1, NCP, HEAD_DIM), lambda b, g: (b, g, 0, 0))
    v_spec = pl.BlockSpec((1, 1, HEAD_DIM, NCP), lambda b, g: (b, g, 0, 0))
    return pl.pallas_call(
        functools.partial(_compress_body, NCP=NCP),
        out_shape=(k_sds, v_sds),
        grid=(B, NSA_KV_GROUPS),
        in_specs=[pl.BlockSpec((T, LANES), lambda b, g: (b, CB_NKC + g)),
                  pl.BlockSpec((T, LANES), lambda b, g: (b, CB_NVC + g)),
                  full((CMP_LEN, HEAD_DIM)), full((CMP_LEN * HEAD_DIM, CMP_HIDDEN)),
                  full((CMP_HIDDEN, HEAD_DIM)),
                  full((CMP_LEN, HEAD_DIM)), full((CMP_LEN * HEAD_DIM, CMP_HIDDEN)),
                  full((CMP_HIDDEN, HEAD_DIM)),
                  full((NCP, LANES)), full((NCP, LANES))],
        out_specs=(k_spec, v_spec),
        compiler_params=_cparams(2),
        name="compress",
    )(proj, proj, pek, w1k, w2k, pev, w1v, w2v, cc, sc)


def _nsa_body(q_ref, sm_ref, cq_ref, sq_ref, kc_ref, vc_ref, ks_ref, vs_ref, kw_ref, vw_ref,
              ct_ref, st_ref, ovt_ref, o_ref,
              ksa_ref, vst_ref, kwb_ref, vwt_ref, smt_ref, m_ref, l_ref, acc_ref, sa_ref, sb_ref,
              part_ref, *, T, TQ, TK, WSP, NB, NCP, NSEL):
    i = pl.program_id(1)
    HG = NSA_GROUP_SIZE
    NG = NSA_KV_GROUPS
    groups = range(NG)
    R = HG * TQ
    RC = 512

    @pl.when(i == 0)
    def _():
        for c0 in range(0, T, RC):
            sl = slice(c0, c0 + RC)
            ct = ct_ref[sl, :]
            st = st_ref[sl, :]
            kblk = (c0 + lax.broadcasted_iota(jnp.int32, (RC, LANES), 0)) >> 6
            onehot = jnp.where(kblk == lax.broadcasted_iota(jnp.int32, (RC, LANES), 1), 1.0, 0.0)
            for gg in groups:
                gc = slice(gg * HEAD_DIM, (gg + 1) * HEAD_DIM)
                ksa_ref[gg, sl, 0:HEAD_DIM] = _rope(ks_ref[sl, gc], ct, st).astype(MXU_DT)
                ksa_ref[gg, sl, HEAD_DIM:HEAD_DIM + LANES] = onehot.astype(MXU_DT)
                kwb_ref[gg, sl, :] = _rope(kw_ref[sl, gc], ct, st).astype(MXU_DT)
                vst_ref[gg, :, sl] = vs_ref[sl, gc].T.astype(MXU_DT)
                vwt_ref[gg, :, sl] = vw_ref[sl, gc].T.astype(MXU_DT)

    q0 = i * TQ
    cq = cq_ref[...]
    sq = sq_ref[...]
    qpos = q0 + (lax.broadcasted_iota(jnp.int32, (1, R), 1) & (TQ - 1))
    smt_ref[...] = sm_ref[...].T

    def gate_row(gg, c):
        return jnp.concatenate(
            [_sigmoid(smt_ref[SM_GATE + 3 * (gg * HG + hh) + c:SM_GATE + 3 * (gg * HG + hh) + c + 1, :])
             for hh in range(HG)], axis=1)

    w0 = pl.multiple_of(jnp.clip((i + 1) * TQ - WSP, 0, T - WSP), LANES)
    dist = (q0 + lax.broadcasted_iota(jnp.int32, (WSP, TQ), 1)) \
        - (w0 + lax.broadcasted_iota(jnp.int32, (WSP, TQ), 0))
    wbias = jnp.where((dist >= 0) & (dist < WINDOW), 0.0, NEG_INF)
    wbias = jnp.concatenate([wbias] * HG, axis=1)
    ovt = ovt_ref[...]
    jblk = lax.broadcasted_iota(jnp.int32, (NB, TQ), 0)
    cur = (q0 + lax.broadcasted_iota(jnp.int32, (NB, TQ), 1)) >> 6
    forced = (jblk == 0) | (jblk == cur) | (jblk == cur - 1)
    future = jblk > cur
    SUB = 8
    sub = lax.broadcasted_iota(jnp.int32, (SUB, TQ), 0)

    qa = []
    for gg in groups:
        qb = jnp.concatenate(
            [_rope(q_ref[:, (gg * HG + hh) * HEAD_DIM:(gg * HG + hh + 1) * HEAD_DIM], cq, sq) * QSCALE
             for hh in range(HG)], axis=0).astype(MXU_DT)

        sc = _dot_nt(kc_ref[0, gg], qb)
        nrow = lax.broadcasted_iota(jnp.int32, (NCP, R), 0)
        last_valid = (qpos - (CMP_LEN - 1)) >> 4
        sc = jnp.where(nrow <= last_valid, sc, NEG_INF)
        pc = jnp.exp2(sc - jnp.max(sc, axis=0, keepdims=True))
        pc = pc * (jnp.where(qpos >= CMP_LEN - 1, 1.0, 0.0) / jnp.sum(pc, axis=0, keepdims=True))
        o_c = _dot(vc_ref[0, gg], pc.astype(MXU_DT))

        sw = _dot_nt(kwb_ref[gg, pl.ds(w0, WSP), :], qb) + wbias
        pw = jnp.exp2(sw - jnp.max(sw, axis=0, keepdims=True))
        o_w = _dot(vwt_ref[gg, :, pl.ds(w0, WSP)], pw.astype(MXU_DT)) / jnp.sum(pw, axis=0, keepdims=True)
        part_ref[gg] = gate_row(gg, 0) * o_c + gate_row(gg, 2) * o_w

        psum = (pc[:, 0:TQ] + pc[:, TQ:2 * TQ]) + (pc[:, 2 * TQ:3 * TQ] + pc[:, 3 * TQ:4 * TQ])
        hi, mid, lo = _split3(psum)
        imp = (_dot(ovt, hi) + _dot(ovt, mid)) + _dot(ovt, lo)
        imp = jnp.where(forced, FORCE_SCORE, jnp.where(future, -1.0, imp))
        blocks = [imp[r * SUB:(r + 1) * SUB, :] for r in range(NB // SUB)]
        ranks = [jnp.zeros((SUB, TQ), F32) for _ in blocks]
        for jp in range(NB):
            gj, sj = divmod(jp, SUB)
            row = blocks[gj][sj:sj + 1, :]
            for r, blk in enumerate(blocks):
                if r > gj:
                    beats = jnp.where(row >= blk, 1.0, 0.0)
                elif r < gj:
                    beats = jnp.where(row > blk, 1.0, 0.0)
                else:
                    beats = jnp.where(sub > sj, jnp.where(row >= blk, 1.0, 0.0),
                                      jnp.where(row > blk, 1.0, 0.0))
                ranks[r] = ranks[r] + beats
        rank = jnp.concatenate(ranks, axis=0)
        sel_t = jnp.where((rank < NSEL) & (imp >= 0.0), 1.0, 0.0)
        if NB < LANES:
            sel_t = jnp.concatenate([sel_t, jnp.zeros((LANES - NB, TQ), F32)], axis=0)
        selneg = jnp.where(sel_t.T > 0.5, 0.0, NEG_INF).astype(MXU_DT)
        qa.append(jnp.concatenate([qb, jnp.concatenate([selneg] * HG, axis=0)], axis=1))

    m_ref[...] = jnp.full(m_ref.shape, NEG_INF, F32)
    l_ref[...] = jnp.zeros(l_ref.shape, F32)
    acc_ref[...] = jnp.zeros(acc_ref.shape, F32)

    def scores(t, s_ref):
        k0 = pl.multiple_of(t * TK, TK)
        for gg in groups:
            s_ref[gg] = _dot_nt(ksa_ref[gg, pl.ds(k0, TK), :], qa[gg])

    def consume(t, s_ref, causal):
        k0 = pl.multiple_of(t * TK, TK)
        for gg in groups:
            if causal:
                own = pl.ds(pl.multiple_of(q0 - k0, TQ), TQ)
                krow = lax.broadcasted_iota(jnp.int32, (TQ, TQ), 0)
                qcol = lax.broadcasted_iota(jnp.int32, (TQ, TQ), 1)
                tri = jnp.where(krow <= qcol, 0.0, NEG_INF)
                s_ref[gg, own, :] = s_ref[gg, own, :] + jnp.concatenate([tri] * HG, axis=1)
            _online_softmax_step_t(s_ref[gg], vst_ref[gg, :, pl.ds(k0, TK)],
                                   m_ref.at[gg], l_ref.at[gg], acc_ref.at[gg])

    last = ((i + 1) * TQ + TK - 1) // TK - 1
    scores(0, sa_ref)

    def pair(jj, carry):
        scores(2 * jj + 1, sb_ref)
        consume(2 * jj, sa_ref, False)
        scores(2 * jj + 2, sa_ref)
        consume(2 * jj + 1, sb_ref, False)
        return carry

    lax.fori_loop(0, last // 2, pair, 0)

    @pl.when(last % 2 == 0)
    def _():
        consume(last, sa_ref, True)

    @pl.when(last % 2 == 1)
    def _():
        scores(last, sb_ref)
        consume(last - 1, sa_ref, False)
        consume(last, sb_ref, True)

    for gg in groups:
        o = part_ref[gg] + (gate_row(gg, 1) / l_ref[gg]) * acc_ref[gg]
        for hh in range(HG):
            oc = (gg * HG + hh) * HEAD_DIM
            o_ref[:, oc:oc + HEAD_DIM] = o[:, hh * TQ:(hh + 1) * TQ].T.astype(o_ref.dtype)


def _nsa(proj, kc, vc, ct, st, ovt, B, T):
    TQ = 128
    TK = 512
    WSP = WINDOW + TQ
    NB = T // SLC_LEN
    NCP = T // CMP_STRIDE
    nq = T // TQ
    HG = NSA_GROUP_SIZE
    NG = NSA_KV_GROUPS
    R = HG * TQ
    one = pl.Buffered(1)
    kv_spec = lambda cb: pl.BlockSpec((T, NG * LANES), lambda b, i: (b, cb // NG), pipeline_mode=one)
    return pl.pallas_call(
        functools.partial(_nsa_body, T=T, TQ=TQ, TK=TK, WSP=WSP, NB=NB, NCP=NCP,
                          NSEL=min(SLC_TOP, NB)),
        out_shape=jax.ShapeDtypeStruct((B * T, NSA_W), MXU_DT),
        grid=(B, nq),
        in_specs=[pl.BlockSpec((TQ, NSA_W), lambda b, i: (b * nq + i, CB_NQ * LANES // NSA_W)),
                  pl.BlockSpec((TQ, LANES), lambda b, i: (b * nq + i, CB_SM)),
                  pl.BlockSpec((TQ, LANES), lambda b, i: (i, 0)),
                  pl.BlockSpec((TQ, LANES), lambda b, i: (i, 0)),
                  pl.BlockSpec((1, NG, NCP, HEAD_DIM), lambda b, i: (b, 0, 0, 0)),
                  pl.BlockSpec((1, NG, HEAD_DIM, NCP), lambda b, i: (b, 0, 0, 0)),
                  kv_spec(CB_NKS), kv_spec(CB_NVS), kv_spec(CB_NKW), kv_spec(CB_NVW),
                  pl.BlockSpec((T, LANES), lambda b, i: (0, 0), pipeline_mode=one),
                  pl.BlockSpec((T, LANES), lambda b, i: (0, 0), pipeline_mode=one),
                  pl.BlockSpec((NB, NCP), lambda b, i: (0, 0))],
        out_specs=pl.BlockSpec((TQ, NSA_W), lambda b, i: (b * nq + i, 0)),
        scratch_shapes=[pltpu.VMEM((NG, T, HEAD_DIM + LANES), MXU_DT),
                        pltpu.VMEM((NG, HEAD_DIM, T), MXU_DT),
                        pltpu.VMEM((NG, T, LANES), MXU_DT),
                        pltpu.VMEM((NG, HEAD_DIM, T), MXU_DT),
                        pltpu.VMEM((LANES, TQ), F32),
                        pltpu.VMEM((NG, 1, R), F32),
                        pltpu.VMEM((NG, 1, R), F32),
                        pltpu.VMEM((NG, HEAD_DIM, R), F32),
                        pltpu.VMEM((NG, TK, R), F32),
                        pltpu.VMEM((NG, TK, R), F32),
                        pltpu.VMEM((NG, HEAD_DIM, R), F32)],
        compiler_params=_cparams(2),
        name="nsa",
    )(proj, proj, ct, st, kc, vc, proj, proj, proj, proj, ct, st, ovt)


def _merge_body(of_ref, on_ref, wf_ref, wn_ref, gf_ref, gn_ref, o_ref):
    yf = _dot(of_ref[...], wf_ref[...])
    yn = _dot(on_ref[...], wn_ref[...])
    o_ref[...] = (_sigmoid(gf_ref[...]) * yf + _sigmoid(gn_ref[...]) * yn).astype(o_ref.dtype)


def _merge(o_fox, o_nsa, wf, wn, layer, proj):
    m = o_fox.shape[0]
    tm = min(1024, m)
    tn = 512
    gfb = CB_GF * LANES // tn
    gnb = CB_GN * LANES // tn
    return pl.pallas_call(
        _merge_body,
        out_shape=jax.ShapeDtypeStruct((m, D_MODEL), MXU_DT),
        grid=(m // tm, D_MODEL // tn),
        in_specs=[pl.BlockSpec((tm, FOX_W), lambda i, j: (i, 0)),
                  pl.BlockSpec((tm, NSA_W), lambda i, j: (i, 0)),
                  pl.BlockSpec((None, FOX_W, tn), lambda i, j: (layer, 0, j)),
                  pl.BlockSpec((None, NSA_W, tn), lambda i, j: (layer, 0, j)),
                  pl.BlockSpec((tm, tn), lambda i, j: (i, gfb + j)),
                  pl.BlockSpec((tm, tn), lambda i, j: (i, gnb + j))],
        out_specs=pl.BlockSpec((tm, tn), lambda i, j: (i, j)),
        compiler_params=_cparams(2),
        name="merge",
    )(o_fox, o_nsa, wf, wn, proj, proj)


def _outproj_body(a_ref, w_ref, g_ref, x_ref, o_ref):
    y = _dot(a_ref[...], w_ref[...])
    o_ref[...] = x_ref[...] + _rms(y) * g_ref[...]


def _outproj(mix, w, layer, g, x2):
    m, d = x2.shape
    tm = min(512, m)
    return pl.pallas_call(
        _outproj_body,
        out_shape=jax.ShapeDtypeStruct((m, d), F32),
        grid=(m // tm,),
        in_specs=[pl.BlockSpec((tm, d), lambda i: (i, 0)),
                  pl.BlockSpec((None, d, d), lambda i: (layer, 0, 0)),
                  pl.BlockSpec((1, d), lambda i: (0, 0)),
                  pl.BlockSpec((tm, d), lambda i: (i, 0))],
        out_specs=pl.BlockSpec((tm, d), lambda i: (i, 0)),
        compiler_params=_cparams(1),
        name="outproj",
    )(mix, w, g, x2)


def _ffn_up_body(x_ref, g_ref, wg_ref, wu_ref, o_ref, a_ref):
    @pl.when(pl.program_id(1) == 0)
    def _():
        a_ref[...] = (_rms(x_ref[...]) * g_ref[...]).astype(a_ref.dtype)

    a = a_ref[...]
    hg = _dot(a, wg_ref[...])
    hu = _dot(a, wu_ref[...])
    o_ref[...] = (hg * _sigmoid(hg) * hu).astype(o_ref.dtype)


def _ffn_up(x2, g, wg, wu, layer):
    m, d = x2.shape
    f = wg.shape[2]
    tm = min(1024, m)
    tn = 512
    return pl.pallas_call(
        _ffn_up_body,
        out_shape=jax.ShapeDtypeStruct((m, f), MXU_DT),
        grid=(m // tm, f // tn),
        in_specs=[pl.BlockSpec((tm, d), lambda i, j: (i, 0)),
                  pl.BlockSpec((1, d), lambda i, j: (0, 0)),
                  pl.BlockSpec((None, d, tn), lambda i, j: (layer, 0, j)),
                  pl.BlockSpec((None, d, tn), lambda i, j: (layer, 0, j))],
        out_specs=pl.BlockSpec((tm, tn), lambda i, j: (i, j)),
        scratch_shapes=[pltpu.VMEM((tm, d), MXU_DT)],
        compiler_params=_cparams(2),
        name="ffn_up",
    )(x2, g, wg, wu)


def _ffn_down_body(h_ref, w_ref, g_ref, x_ref, o_ref):
    y = _dot(h_ref[...], w_ref[...])
    o_ref[...] = x_ref[...] + _rms(y) * g_ref[...]


def _ffn_down(h, w, layer, g, x2):
    m, d = x2.shape
    f = h.shape[1]
    tm = min(256, m)
    return pl.pallas_call(
        _ffn_down_body,
        out_shape=jax.ShapeDtypeStruct((m, d), F32),
        grid=(m // tm,),
        in_specs=[pl.BlockSpec((tm, f), lambda i: (i, 0)),
                  pl.BlockSpec((None, f, d), lambda i: (layer, 0, 0), pipeline_mode=pl.Buffered(1)),
                  pl.BlockSpec((1, d), lambda i: (0, 0)),
                  pl.BlockSpec((tm, d), lambda i: (i, 0))],
        out_specs=pl.BlockSpec((tm, d), lambda i: (i, 0)),
        compiler_params=_cparams(1),
        name="ffn_down",
    )(h, w, g, x2)


def _pack_w_in(w):
    offs = np.cumsum([0] + IN_SIZES)
    seg = lambda k: w[:, offs[k]:offs[k + 1]]
    small = jnp.concatenate(
        [seg(3), seg(11), jnp.zeros((w.shape[0], LANES - FOX_HEADS - 3 * NSA_HEADS), w.dtype)], axis=1)
    cols = [w[:, offs[0]:offs[3]], w[:, offs[4]:offs[11]], w[:, offs[12]:offs[14]], small]
    return jnp.concatenate(cols, axis=1).astype(MXU_DT)


def _rope_tables(pos):
    inv_freq = jnp.power(ROPE_THETA, -jnp.arange(ROPE_HALF, dtype=F32) * (2.0 / ROPE_DIM))
    ang = pos[:, None] * inv_freq[None, :]
    cos, sin = jnp.cos(ang), jnp.sin(ang)
    n = pos.shape[0]
    pad = HEAD_DIM - ROPE_DIM
    c = jnp.concatenate([cos, cos, jnp.ones((n, pad), F32)], axis=1)
    s = jnp.concatenate([-sin, sin, jnp.zeros((n, pad), F32)], axis=1)
    return c, s


def _overlap_t(T):
    nb, ncp = T // SLC_LEN, T // CMP_STRIDE
    sc = np.arange(ncp) * CMP_STRIDE
    ss = np.arange(nb) * SLC_LEN
    ov = np.minimum(sc[None, :] + CMP_LEN, ss[:, None] + SLC_LEN) - np.maximum(sc[None, :], ss[:, None])
    ov = np.clip(ov, 0, None) / CMP_LEN
    ov[:, ncp - 1] = 0.0
    return jnp.asarray(ov, dtype=MXU_DT)


def _layer(x2, B, T, layer, n_mix_pre, n_mix_post, n_ffn_pre, n_ffn_post, w_in, f_bias,
           ck_pe, ck_w1, ck_w2, cv_pe, cv_w1, cv_w2, stacked, tables):
    ct, st, cc, sc, ovt = tables
    w_up_fox, w_up_nsa, w_out, w_gate, w_up, w_down = stacked
    row = lambda v: v.reshape(1, -1).astype(F32)
    bf = lambda w: w.astype(MXU_DT)
    proj = _inproj(x2, row(n_mix_pre), _pack_w_in(w_in))
    bias_row = jnp.pad(f_bias.astype(F32), (0, LANES - FOX_HEADS)).reshape(1, LANES)
    cum = _foxcum(proj, bias_row, B, T)
    o_fox = _fox(proj, cum, B, T)
    kc, vc = _compress(proj, ck_pe, bf(ck_w1), bf(ck_w2), cv_pe, bf(cv_w1), bf(cv_w2), cc, sc, B, T)
    o_nsa = _nsa(proj, kc, vc, ct, st, ovt, B, T)
    mix = _merge(o_fox, o_nsa, w_up_fox, w_up_nsa, layer, proj)
    x2 = _outproj(mix, w_out, layer, row(n_mix_post), x2)
    h = _ffn_up(x2, row(n_ffn_pre), w_gate, w_up, layer)
    return _ffn_down(h, w_down, layer, row(n_ffn_post), x2)


@jax.jit
def kernel(x, norm_mix_pre, norm_mix_post, norm_ffn_pre, norm_ffn_post, w_in, fox_forget_bias, cmp_k_pe, cmp_k_w1, cmp_k_w2, cmp_v_pe, cmp_v_w1, cmp_v_w2, w_up_fox, w_up_nsa, w_out, w_ffn_gate, w_ffn_up, w_ffn_down):
    B, T, D = x.shape
    ct, st = _rope_tables(jnp.arange(T, dtype=F32))
    ncp = T // CMP_STRIDE
    cc, sc = _rope_tables((jnp.arange(ncp) * CMP_STRIDE + CMP_LEN - 1).astype(F32))
    tables = (ct, st, cc, sc, _overlap_t(T))
    x2 = x.reshape(B * T, D)
    stacked = tuple(w.astype(MXU_DT)
                    for w in (w_up_fox, w_up_nsa, w_out, w_ffn_gate, w_ffn_up, w_ffn_down))
    for l in range(w_in.shape[0]):
        x2 = _layer(x2, B, T, l, norm_mix_pre[l], norm_mix_post[l], norm_ffn_pre[l], norm_ffn_post[l],
                    w_in[l], fox_forget_bias[l],
                    cmp_k_pe[l], cmp_k_w1[l], cmp_k_w2[l], cmp_v_pe[l], cmp_v_w1[l], cmp_v_w2[l],
                    stacked, tables)
    return x2.reshape(B, T, D)
```

```python
import functools

import numpy as np
import jax
import jax.numpy as jnp
from jax import lax
from jax.experimental import pallas as pl
from jax.experimental.pallas import tpu as pltpu

D_MODEL = 2048
HEAD_DIM = 128
FOX_HEADS = 8
NSA_HEADS = 8
NSA_GROUP_SIZE = 4
NSA_KV_GROUPS = 2
FOX_W = FOX_HEADS * HEAD_DIM
NSA_W = NSA_HEADS * HEAD_DIM
NSA_KV_W = NSA_KV_GROUPS * HEAD_DIM
ROPE_DIM = HEAD_DIM // 4
ROPE_HALF = ROPE_DIM // 2
ROPE_THETA = 500000.0
CMP_LEN = 32
CMP_STRIDE = 16
CMP_HIDDEN = 256
SLC_LEN = 64
SLC_TOP = 16
WINDOW = 512
FFN_HIDDEN = 5632
EPS = 1e-6
NEG_INF = -1e30
FORCE_SCORE = 1e6
SCALE = HEAD_DIM ** -0.5
LOG2E = float(np.log2(np.e))
QSCALE = SCALE * LOG2E
IN_SIZES = [FOX_W, FOX_W, FOX_W, FOX_HEADS, NSA_W,
            NSA_KV_W, NSA_KV_W, NSA_KV_W, NSA_KV_W, NSA_KV_W, NSA_KV_W,
            NSA_HEADS * 3, D_MODEL, D_MODEL]

LANES = 128
MXU_DT = jnp.bfloat16
VMEM_LIMIT = 56 * 1024 * 1024

CB_GF, CB_GN = 0, 16
CB_FQ, CB_FK, CB_FV, CB_NQ = 32, 40, 48, 56
CB_NKC, CB_NVC, CB_NKS, CB_NVS, CB_NKW, CB_NVW = 64, 66, 68, 70, 72, 74
CB_SM = 76
NP_COLS = 78 * LANES
SM_FF, SM_GATE = 0, FOX_HEADS
MXU_N = 256

F32 = jnp.float32


def _cparams(n_axes):
    return pltpu.CompilerParams(dimension_semantics=("arbitrary",) * n_axes,
                                vmem_limit_bytes=VMEM_LIMIT)


def _rms(x):
    return x * lax.rsqrt(jnp.mean(x * x, axis=-1, keepdims=True) + EPS)


def _dot(a, b):
    return jnp.dot(a, b, preferred_element_type=F32)


def _dot_nt(a, b):
    return lax.dot_general(a, b, (((1,), (1,)), ((), ())), preferred_element_type=F32)


def _split3(x):
    hi = x.astype(MXU_DT)
    r1 = x - hi.astype(F32)
    mid = r1.astype(MXU_DT)
    lo = (r1 - mid.astype(F32)).astype(MXU_DT)
    return hi, mid, lo


def _rope(x, c, s):
    lane = lax.broadcasted_iota(jnp.int32, x.shape, 1)
    partner = jnp.where(lane < ROPE_HALF,
                        pltpu.roll(x, LANES - ROPE_HALF, 1),
                        pltpu.roll(x, ROPE_HALF, 1))
    return x * c + partner * s


def _sigmoid(z):
    return 1.0 / (1.0 + jnp.exp(-z))


def _inproj_body(x_ref, g_ref, w_ref, o_ref, a_ref):
    @pl.when(pl.program_id(1) == 0)
    def _():
        a_ref[...] = (_rms(x_ref[...]) * g_ref[...]).astype(a_ref.dtype)

    o_ref[...] = _dot(a_ref[...], w_ref[...])


def _inproj(x2, g, w):
    m, d = x2.shape
    n = w.shape[1]
    tm = min(1024, m)
    tn = 3 * MXU_N
    return pl.pallas_call(
        _inproj_body,
        out_shape=jax.ShapeDtypeStruct((m, n), F32),
        grid=(m // tm, n // tn),
        in_specs=[pl.BlockSpec((tm, d), lambda i, j: (i, 0)),
                  pl.BlockSpec((1, d), lambda i, j: (0, 0)),
                  pl.BlockSpec((d, tn), lambda i, j: (0, j))],
        out_specs=pl.BlockSpec((tm, tn), lambda i, j: (i, j)),
        scratch_shapes=[pltpu.VMEM((tm, d), MXU_DT)],
        compiler_params=_cparams(2),
        name="inproj",
    )(x2, g, w)


def _foxcum_body(s_ref, b_ref, cum_ref, *, T, CH):
    r = lax.broadcasted_iota(jnp.int32, (CH, CH), 0)
    c = lax.broadcasted_iota(jnp.int32, (CH, CH), 1)
    tri = jnp.where(c <= r, 1.0, 0.0).astype(MXU_DT)
    carry = jnp.zeros((1, LANES), F32)
    for ci in range(T // CH):
        z = s_ref[ci * CH:(ci + 1) * CH, :] + b_ref[...]
        lf = jnp.minimum(z, 0.0) - jnp.log1p(jnp.exp(-jnp.abs(z)))
        hi, mid, lo = _split3(lf)
        out = (_dot(tri, hi) + _dot(tri, mid)) + _dot(tri, lo) + carry
        cum_ref[ci * CH:(ci + 1) * CH, :] = out
        carry = out[CH - 1:CH, :]


def _foxcum(proj, bias_row, B, T):
    CH = 256
    return pl.pallas_call(
        functools.partial(_foxcum_body, T=T, CH=CH),
        out_shape=jax.ShapeDtypeStruct((B * T, LANES), F32),
        grid=(B,),
        in_specs=[pl.BlockSpec((T, LANES), lambda b: (b, CB_SM)),
                  pl.BlockSpec((1, LANES), lambda b: (0, 0))],
        out_specs=pl.BlockSpec((T, LANES), lambda b: (b, 0)),
        compiler_params=_cparams(1),
        name="foxcum",
    )(proj, bias_row)


def _online_softmax_step_t(st, vt_t, m_ref, l_ref, acc_ref):
    m_old = m_ref[...]
    m_new = jnp.maximum(m_old, jnp.max(st, axis=0, keepdims=True))
    alpha = jnp.exp2(m_old - m_new)
    p = jnp.exp2(st - m_new)
    l_ref[...] = alpha * l_ref[...] + jnp.sum(p, axis=0, keepdims=True)
    acc_ref[...] = alpha * acc_ref[...] + _dot(vt_t, p.astype(MXU_DT))
    m_ref[...] = m_new


def _fox_body(q_ref, k_ref, v_ref, cum_ref, o_ref,
              ka_ref, vt_ref, m_ref, l_ref, acc_ref, sa_ref, sb_ref, *, T, TQ, NH):
    hp = pl.program_id(1)
    i = pl.program_id(2)
    RC = 512
    heads = range(NH)

    def cum_pieces(hh, rows, n):
        lane = lax.broadcasted_iota(jnp.int32, (n, LANES), 1)
        col = jnp.sum(jnp.where(lane == hp * NH + hh, cum_ref[rows, :], 0.0), axis=1, keepdims=True)
        return [p.astype(F32) for p in _split3(col * LOG2E)]

    @pl.when(i == 0)
    def _():
        for hh in heads:
            hcols = slice(hh * HEAD_DIM, (hh + 1) * HEAD_DIM)
            for c0 in range(0, T, RC):
                sl = slice(c0, c0 + RC)
                ka_ref[hh, sl, 0:HEAD_DIM] = k_ref[sl, hcols].astype(MXU_DT)
                lane = lax.broadcasted_iota(jnp.int32, (RC, LANES), 1)
                aug = jnp.where((lane >= 3) & (lane < 6), 1.0, 0.0)
                for idx, c in enumerate(cum_pieces(hh, sl, RC)):
                    aug = jnp.where(lane == idx, c, aug)
                ka_ref[hh, sl, HEAD_DIM:HEAD_DIM + LANES] = aug.astype(MXU_DT)
                vt_ref[hh, :, sl] = v_ref[sl, hcols].T.astype(MXU_DT)

    q0 = pl.multiple_of(i * TQ, TQ)
    lane = lax.broadcasted_iota(jnp.int32, (TQ, LANES), 1)
    qa = []
    for hh in heads:
        qaug = jnp.where(lane < 3, -1.0, 0.0)
        for idx, c in enumerate(cum_pieces(hh, pl.ds(q0, TQ), TQ)):
            qaug = jnp.where(lane == 3 + idx, c, qaug)
        qh = q_ref[:, hh * HEAD_DIM:(hh + 1) * HEAD_DIM] * QSCALE
        qa.append(jnp.concatenate([qh.astype(MXU_DT), qaug.astype(MXU_DT)], axis=1))
    m_ref[...] = jnp.full(m_ref.shape, NEG_INF, F32)
    l_ref[...] = jnp.zeros(l_ref.shape, F32)
    acc_ref[...] = jnp.zeros(acc_ref.shape, F32)

    def scores(t, s_ref):
        k0 = pl.multiple_of(t * TQ, TQ)
        for hh in heads:
            s_ref[hh] = _dot_nt(ka_ref[hh, pl.ds(k0, TQ), :], qa[hh])

    def consume(t, s_ref, diagonal):
        k0 = pl.multiple_of(t * TQ, TQ)
        for hh in heads:
            st = s_ref[hh]
            if diagonal:
                krow = lax.broadcasted_iota(jnp.int32, (TQ, TQ), 0)
                qcol = lax.broadcasted_iota(jnp.int32, (TQ, TQ), 1)
                st = jnp.where(krow <= qcol, st, NEG_INF)
            _online_softmax_step_t(st, vt_ref[hh, :, pl.ds(k0, TQ)],
                                   m_ref.at[hh], l_ref.at[hh], acc_ref.at[hh])

    scores(0, sa_ref)

    def pair(jj, carry):
        scores(2 * jj + 1, sb_ref)
        consume(2 * jj, sa_ref, False)
        scores(2 * jj + 2, sa_ref)
        consume(2 * jj + 1, sb_ref, False)
        return carry

    lax.fori_loop(0, i // 2, pair, 0)

    @pl.when(i % 2 == 0)
    def _():
        consume(i, sa_ref, True)

    @pl.when(i % 2 == 1)
    def _():
        scores(i, sb_ref)
        consume(i - 1, sa_ref, False)
        consume(i, sb_ref, True)

    for hh in heads:
        o_ref[:, hh * HEAD_DIM:(hh + 1) * HEAD_DIM] = (acc_ref[hh] / l_ref[hh]).T.astype(o_ref.dtype)


def _fox(proj, cum, B, T):
    TQ = 512
    NH = 4
    nq = T // TQ
    hw = NH * HEAD_DIM
    one = pl.Buffered(1)
    return pl.pallas_call(
        functools.partial(_fox_body, T=T, TQ=TQ, NH=NH),
        out_shape=jax.ShapeDtypeStruct((B * T, FOX_W), MXU_DT),
        grid=(B, FOX_HEADS // NH, nq),
        in_specs=[pl.BlockSpec((TQ, hw), lambda b, h, i: (b * nq + i, CB_FQ // NH + h)),
                  pl.BlockSpec((T, hw), lambda b, h, i: (b, CB_FK // NH + h), pipeline_mode=one),
                  pl.BlockSpec((T, hw), lambda b, h, i: (b, CB_FV // NH + h), pipeline_mode=one),
                  pl.BlockSpec((T, LANES), lambda b, h, i: (b, 0), pipeline_mode=one)],
        out_specs=pl.BlockSpec((TQ, hw), lambda b, h, i: (b * nq + i, h)),
        scratch_shapes=[pltpu.VMEM((NH, T, HEAD_DIM + LANES), MXU_DT),
                        pltpu.VMEM((NH, HEAD_DIM, T), MXU_DT),
                        pltpu.VMEM((NH, 1, TQ), F32),
                        pltpu.VMEM((NH, 1, TQ), F32),
                        pltpu.VMEM((NH, HEAD_DIM, TQ), F32),
                        pltpu.VMEM((NH, TQ, TQ), F32),
                        pltpu.VMEM((NH, TQ, TQ), F32)],
        compiler_params=_cparams(3),
        name="fox",
    )(proj, proj, proj, cum)


def _gelu_tanh(x):
    return 0.5 * x * (1.0 + jnp.tanh(np.sqrt(2.0 / np.pi).astype(np.float32)
                                     * (x + 0.044715 * (x * x * x))))


def _compress_one(x_ref, pe_ref, w1_ref, w2_ref, NCP):
    h0 = jnp.zeros((NCP, CMP_HIDDEN), F32)
    h1 = jnp.zeros((NCP, CMP_HIDDEN), F32)
    for l in range(CMP_STRIDE):
        xl = x_ref[pl.ds(l, NCP, stride=CMP_STRIDE), :]
        a0 = (xl + pe_ref[l:l + 1, :]).astype(MXU_DT)
        a1 = (xl + pe_ref[CMP_STRIDE + l:CMP_STRIDE + l + 1, :]).astype(MXU_DT)
        h0 = h0 + _dot(a0, w1_ref[l * HEAD_DIM:(l + 1) * HEAD_DIM, :])
        h1 = h1 + _dot(a1, w1_ref[(CMP_STRIDE + l) * HEAD_DIM:(CMP_STRIDE + l + 1) * HEAD_DIM, :])
    hsum = h0 + pltpu.roll(h1, NCP - 1, 0)
    return _dot(_gelu_tanh(hsum).astype(MXU_DT), w2_ref[...])


def _compress_body(xk_ref, xv_ref, pek_ref, w1k_ref, w2k_ref, pev_ref, w1v_ref, w2v_ref,
                   cc_ref, sc_ref, kc_ref, vc_ref, *, NCP):
    kc = _compress_one(xk_ref, pek_ref, w1k_ref, w2k_ref, NCP)
    kc_ref[0, 0] = _rope(kc, cc_ref[...], sc_ref[...]).astype(kc_ref.dtype)
    vc = _compress_one(xv_ref, pev_ref, w1v_ref, w2v_ref, NCP)
    vc_ref[0, 0] = vc.T.astype(vc_ref.dtype)


def _compress(proj, pek, w1k, w2k, pev, w1v, w2v, cc, sc, B, T):
    NCP = T // CMP_STRIDE
    full = lambda shape: pl.BlockSpec(shape, lambda b, g: (0,) * len(shape))
    k_sds = jax.ShapeDtypeStruct((B, NSA_KV_GROUPS, NCP, HEAD_DIM), MXU_DT)
    v_sds = jax.ShapeDtypeStruct((B, NSA_KV_GROUPS, HEAD_DIM, NCP), MXU_DT)
    k_spec = pl.BlockSpec((1, 1, NCP, HEAD_DIM), lambda b, g: (b, g, 0, 0))
    v_spec = pl.BlockSpec((1, 1, HEAD_DIM, NCP), lambda b, g: (b, g, 0, 0))
    return pl.pallas_call(
        functools.partial(_compress_body, NCP=NCP),
        out_shape=(k_sds, v_sds),
        grid=(B, NSA_KV_GROUPS),
        in_specs=[pl.BlockSpec((T, LANES), lambda b, g: (b, CB_NKC + g)),
                  pl.BlockSpec((T, LANES), lambda b, g: (b, CB_NVC + g)),
                  full((CMP_LEN, HEAD_DIM)), full((CMP_LEN * HEAD_DIM, CMP_HIDDEN)),
                  full((CMP_HIDDEN, HEAD_DIM)),
                  full((CMP_LEN, HEAD_DIM)), full((CMP_LEN * HEAD_DIM, CMP_HIDDEN)),
                  full((CMP_HIDDEN, HEAD_DIM)),
                  full((NCP, LANES)), full((NCP, LANES))],
        out_specs=(k_spec, v_spec),
        compiler_params=_cparams(2),
        name="compress",
    )(proj, proj, pek, w1k, w2k, pev, w1v, w2v, cc, sc)


def _nsa_body(q_ref, sm_ref, cq_ref, sq_ref, kc_ref, vc_ref, ks_ref, vs_ref, kw_ref, vw_ref,
              ct_ref, st_ref, ovt_ref, o_ref,
              ksa_ref, vst_ref, kwb_ref, vwt_ref, smt_ref, m_ref, l_ref, acc_ref, sa_ref, sb_ref,
              part_ref, *, T, TQ, TK, WSP, NB, NCP, NSEL):
    i = pl.program_id(1)
    HG = NSA_GROUP_SIZE
    NG = NSA_KV_GROUPS
    groups = range(NG)
    R = HG * TQ
    RC = 512

    @pl.when(i == 0)
    def _():
        for c0 in range(0, T, RC):
            sl = slice(c0, c0 + RC)
            ct = ct_ref[sl, :]
            st = st_ref[sl, :]
            kblk = (c0 + lax.broadcasted_iota(jnp.int32, (RC, LANES), 0)) >> 6
            onehot = jnp.where(kblk == lax.broadcasted_iota(jnp.int32, (RC, LANES), 1), 1.0, 0.0)
            for gg in groups:
                gc = slice(gg * HEAD_DIM, (gg + 1) * HEAD_DIM)
                ksa_ref[gg, sl, 0:HEAD_DIM] = _rope(ks_ref[sl, gc], ct, st).astype(MXU_DT)
                ksa_ref[gg, sl, HEAD_DIM:HEAD_DIM + LANES] = onehot.astype(MXU_DT)
                kwb_ref[gg, sl, :] = _rope(kw_ref[sl, gc], ct, st).astype(MXU_DT)
                vst_ref[gg, :, sl] = vs_ref[sl, gc].T.astype(MXU_DT)
                vwt_ref[gg, :, sl] = vw_ref[sl, gc].T.astype(MXU_DT)

    q0 = i * TQ
    cq = cq_ref[...]
    sq = sq_ref[...]
    qpos = q0 + (lax.broadcasted_iota(jnp.int32, (1, R), 1) & (TQ - 1))
    smt_ref[...] = sm_ref[...].T

    def gate_row(gg, c):
        return jnp.concatenate(
            [_sigmoid(smt_ref[SM_GATE + 3 * (gg * HG + hh) + c:SM_GATE + 3 * (gg * HG + hh) + c + 1, :])
             for hh in range(HG)], axis=1)

    w0 = pl.multiple_of(jnp.clip((i + 1) * TQ - WSP, 0, T - WSP), LANES)
    dist = (q0 + lax.broadcasted_iota(jnp.int32, (WSP, TQ), 1)) \
        - (w0 + lax.broadcasted_iota(jnp.int32, (WSP, TQ), 0))
    wbias = jnp.where((dist >= 0) & (dist < WINDOW), 0.0, NEG_INF)
    wbias = jnp.concatenate([wbias] * HG, axis=1)
    ovt = ovt_ref[...]
    jblk = lax.broadcasted_iota(jnp.int32, (NB, TQ), 0)
    cur = (q0 + lax.broadcasted_iota(jnp.int32, (NB, TQ), 1)) >> 6
    forced = (jblk == 0) | (jblk == cur) | (jblk == cur - 1)
    future = jblk > cur
    SUB = 8
    sub = lax.broadcasted_iota(jnp.int32, (SUB, TQ), 0)

    qa = []
    for gg in groups:
        qb = jnp.concatenate(
            [_rope(q_ref[:, (gg * HG + hh) * HEAD_DIM:(gg * HG + hh + 1) * HEAD_DIM], cq, sq) * QSCALE
             for hh in range(HG)], axis=0).astype(MXU_DT)

        sc = _dot_nt(kc_ref[0, gg], qb)
        nrow = lax.broadcasted_iota(jnp.int32, (NCP, R), 0)
        last_valid = (qpos - (CMP_LEN - 1)) >> 4
        sc = jnp.where(nrow <= last_valid, sc, NEG_INF)
        pc = jnp.exp2(sc - jnp.max(sc, axis=0, keepdims=True))
        pc = pc * (jnp.where(qpos >= CMP_LEN - 1, 1.0, 0.0) / jnp.sum(pc, axis=0, keepdims=True))
        o_c = _dot(vc_ref[0, gg], pc.astype(MXU_DT))

        sw = _dot_nt(kwb_ref[gg, pl.ds(w0, WSP), :], qb) + wbias
        pw = jnp.exp2(sw - jnp.max(sw, axis=0, keepdims=True))
        o_w = _dot(vwt_ref[gg, :, pl.ds(w0, WSP)], pw.astype(MXU_DT)) / jnp.sum(pw, axis=0, keepdims=True)
        part_ref[gg] = gate_row(gg, 0) * o_c + gate_row(gg, 2) * o_w

        psum = (pc[:, 0:TQ] + pc[:, TQ:2 * TQ]) + (pc[:, 2 * TQ:3 * TQ] + pc[:, 3 * TQ:4 * TQ])
        hi, mid, lo = _split3(psum)
        imp = (_dot(ovt, hi) + _dot(ovt, mid)) + _dot(ovt, lo)
        imp = jnp.where(forced, FORCE_SCORE, jnp.where(future, -1.0, imp))
        blocks = [imp[r * SUB:(r + 1) * SUB, :] for r in range(NB // SUB)]
        ranks = [jnp.zeros((SUB, TQ), F32) for _ in blocks]
        for jp in range(NB):
            gj, sj = divmod(jp, SUB)
            row = blocks[gj][sj:sj + 1, :]
            for r, blk in enumerate(blocks):
                if r > gj:
                    beats = jnp.where(row >= blk, 1.0, 0.0)
                elif r < gj:
                    beats = jnp.where(row > blk, 1.0, 0.0)
                else:
                    beats = jnp.where(sub > sj, jnp.where(row >= blk, 1.0, 0.0),
                                      jnp.where(row > blk, 1.0, 0.0))
                ranks[r] = ranks[r] + beats
        rank = jnp.concatenate(ranks, axis=0)
        sel_t = jnp.where((rank < NSEL) & (imp >= 0.0), 1.0, 0.0)
        if NB < LANES:
            sel_t = jnp.concatenate([sel_t, jnp.zeros((LANES - NB, TQ), F32)], axis=0)
        selneg = jnp.where(sel_t.T > 0.5, 0.0, NEG_INF).astype(MXU_DT)
        qa.append(jnp.concatenate([qb, jnp.concatenate([selneg] * HG, axis=0)], axis=1))

    m_ref[...] = jnp.full(m_ref.shape, NEG_INF, F32)
    l_ref[...] = jnp.zeros(l_ref.shape, F32)
    acc_ref[...] = jnp.zeros(acc_ref.shape, F32)

    def scores(t, s_ref):
        k0 = pl.multiple_of(t * TK, TK)
        for gg in groups:
            s_ref[gg] = _dot_nt(ksa_ref[gg, pl.ds(k0, TK), :], qa[gg])

    def consume(t, s_ref, causal):
        k0 = pl.multiple_of(t * TK, TK)
        for gg in groups:
            if causal:
                own = pl.ds(pl.multiple_of(q0 - k0, TQ), TQ)
                krow = lax.broadcasted_iota(jnp.int32, (TQ, TQ), 0)
                qcol = lax.broadcasted_iota(jnp.int32, (TQ, TQ), 1)
                tri = jnp.where(krow <= qcol, 0.0, NEG_INF)
                s_ref[gg, own, :] = s_ref[gg, own, :] + jnp.concatenate([tri] * HG, axis=1)
            _online_softmax_step_t(s_ref[gg], vst_ref[gg, :, pl.ds(k0, TK)],
                                   m_ref.at[gg], l_ref.at[gg], acc_ref.at[gg])

    last = ((i + 1) * TQ + TK - 1) // TK - 1
    scores(0, sa_ref)

    def pair(jj, carry):
        scores(2 * jj + 1, sb_ref)
        consume(2 * jj, sa_ref, False)
        scores(2 * jj + 2, sa_ref)
        consume(2 * jj + 1, sb_ref, False)
        return carry

    lax.fori_loop(0, last // 2, pair, 0)

    @pl.when(last % 2 == 0)
    def _():
        consume(last, sa_ref, True)

    @pl.when(last % 2 == 1)
    def _():
        scores(last, sb_ref)
        consume(last - 1, sa_ref, False)
        consume(last, sb_ref, True)

    for gg in groups:
        o = part_ref[gg] + (gate_row(gg, 1) / l_ref[gg]) * acc_ref[gg]
        for hh in range(HG):
            oc = (gg * HG + hh) * HEAD_DIM
            o_ref[:, oc:oc + HEAD_DIM] = o[:, hh * TQ:(hh + 1) * TQ].T.astype(o_ref.dtype)


def _nsa(proj, kc, vc, ct, st, ovt, B, T):
    TQ = 128
    TK = 512
    WSP = WINDOW + TQ
    NB = T // SLC_LEN
    NCP = T // CMP_STRIDE
    nq = T // TQ
    HG = NSA_GROUP_SIZE
    NG = NSA_KV_GROUPS
    R = HG * TQ
    one = pl.Buffered(1)
    kv_spec = lambda cb: pl.BlockSpec((T, NG * LANES), lambda b, i: (b, cb // NG), pipeline_mode=one)
    return pl.pallas_call(
        functools.partial(_nsa_body, T=T, TQ=TQ, TK=TK, WSP=WSP, NB=NB, NCP=NCP,
                          NSEL=min(SLC_TOP, NB)),
        out_shape=jax.ShapeDtypeStruct((B * T, NSA_W), MXU_DT),
        grid=(B, nq),
        in_specs=[pl.BlockSpec((TQ, NSA_W), lambda b, i: (b * nq + i, CB_NQ * LANES // NSA_W)),
                  pl.BlockSpec((TQ, LANES), lambda b, i: (b * nq + i, CB_SM)),
                  pl.BlockSpec((TQ, LANES), lambda b, i: (i, 0)),
                  pl.BlockSpec((TQ, LANES), lambda b, i: (i, 0)),
                  pl.BlockSpec((1, NG, NCP, HEAD_DIM), lambda b, i: (b, 0, 0, 0)),
                  pl.BlockSpec((1, NG, HEAD_DIM, NCP), lambda b, i: (b, 0, 0, 0)),
                  kv_spec(CB_NKS), kv_spec(CB_NVS), kv_spec(CB_NKW), kv_spec(CB_NVW),
                  pl.BlockSpec((T, LANES), lambda b, i: (0, 0), pipeline_mode=one),
                  pl.BlockSpec((T, LANES), lambda b, i: (0, 0), pipeline_mode=one),
                  pl.BlockSpec((NB, NCP), lambda b, i: (0, 0))],
        out_specs=pl.BlockSpec((TQ, NSA_W), lambda b, i: (b * nq + i, 0)),
        scratch_shapes=[pltpu.VMEM((NG, T, HEAD_DIM + LANES), MXU_DT),
                        pltpu.VMEM((NG, HEAD_DIM, T), MXU_DT),
                        pltpu.VMEM((NG, T, LANES), MXU_DT),
                        pltpu.VMEM((NG, HEAD_DIM, T), MXU_DT),
                        pltpu.VMEM((LANES, TQ), F32),
                        pltpu.VMEM((NG, 1, R), F32),
                        pltpu.VMEM((NG, 1, R), F32),
                        pltpu.VMEM((NG, HEAD_DIM, R), F32),
                        pltpu.VMEM((NG, TK, R), F32),
                        pltpu.VMEM((NG, TK, R), F32),
                        pltpu.VMEM((NG, HEAD_DIM, R), F32)],
        compiler_params=_cparams(2),
        name="nsa",
    )(proj, proj, ct, st, kc, vc, proj, proj, proj, proj, ct, st, ovt)


def _merge_body(of_ref, on_ref, wf_ref, wn_ref, gf_ref, gn_ref, o_ref, *, CN):
    of = of_ref[...]
    on = on_ref[...]
    for c0 in range(0, D_MODEL, CN):
        cols = slice(c0, c0 + CN)
        yf = _dot(of, wf_ref[:, cols])
        yn = _dot(on, wn_ref[:, cols])
        o_ref[:, cols] = (_sigmoid(gf_ref[:, cols]) * yf + _sigmoid(gn_ref[:, cols]) * yn).astype(o_ref.dtype)


def _merge(o_fox, o_nsa, wf, wn, layer, proj):
    m = o_fox.shape[0]
    tm = min(512, m)
    d = D_MODEL
    one = pl.Buffered(1)
    return pl.pallas_call(
        functools.partial(_merge_body, CN=2 * MXU_N),
        out_shape=jax.ShapeDtypeStruct((m, d), MXU_DT),
        grid=(m // tm,),
        in_specs=[pl.BlockSpec((tm, FOX_W), lambda i: (i, 0)),
                  pl.BlockSpec((tm, NSA_W), lambda i: (i, 0)),
                  pl.BlockSpec((None, FOX_W, d), lambda i: (layer, 0, 0), pipeline_mode=one),
                  pl.BlockSpec((None, NSA_W, d), lambda i: (layer, 0, 0), pipeline_mode=one),
                  pl.BlockSpec((tm, d), lambda i: (i, CB_GF * LANES // d)),
                  pl.BlockSpec((tm, d), lambda i: (i, CB_GN * LANES // d))],
        out_specs=pl.BlockSpec((tm, d), lambda i: (i, 0)),
        compiler_params=_cparams(1),
        name="merge",
    )(o_fox, o_nsa, wf, wn, proj, proj)


def _outproj_body(a_ref, w_ref, g_ref, x_ref, o_ref):
    y = _dot(a_ref[...], w_ref[...])
    o_ref[...] = x_ref[...] + _rms(y) * g_ref[...]


def _outproj(mix, w, layer, g, x2):
    m, d = x2.shape
    tm = min(512, m)
    return pl.pallas_call(
        _outproj_body,
        out_shape=jax.ShapeDtypeStruct((m, d), F32),
        grid=(m // tm,),
        in_specs=[pl.BlockSpec((tm, d), lambda i: (i, 0)),
                  pl.BlockSpec((None, d, d), lambda i: (layer, 0, 0)),
                  pl.BlockSpec((1, d), lambda i: (0, 0)),
                  pl.BlockSpec((tm, d), lambda i: (i, 0))],
        out_specs=pl.BlockSpec((tm, d), lambda i: (i, 0)),
        compiler_params=_cparams(1),
        name="outproj",
    )(mix, w, g, x2)


def _ffn_up_body(x_ref, g_ref, wg_ref, wu_ref, o_ref, a_ref):
    @pl.when(pl.program_id(1) == 0)
    def _():
        a_ref[...] = (_rms(x_ref[...]) * g_ref[...]).astype(a_ref.dtype)

    a = a_ref[...]
    hg = _dot(a, wg_ref[...])
    hu = _dot(a, wu_ref[...])
    o_ref[...] = (hg * _sigmoid(hg) * hu).astype(o_ref.dtype)


def _ffn_up(x2, g, wg, wu, layer):
    m, d = x2.shape
    f = wg.shape[2]
    tm = min(1024, m)
    tn = 512
    return pl.pallas_call(
        _ffn_up_body,
        out_shape=jax.ShapeDtypeStruct((m, f), MXU_DT),
        grid=(m // tm, f // tn),
        in_specs=[pl.BlockSpec((tm, d), lambda i, j: (i, 0)),
                  pl.BlockSpec((1, d), lambda i, j: (0, 0)),
                  pl.BlockSpec((None, d, tn), lambda i, j: (layer, 0, j)),
                  pl.BlockSpec((None, d, tn), lambda i, j: (layer, 0, j))],
        out_specs=pl.BlockSpec((tm, tn), lambda i, j: (i, j)),
        scratch_shapes=[pltpu.VMEM((tm, d), MXU_DT)],
        compiler_params=_cparams(2),
        name="ffn_up",
    )(x2, g, wg, wu)


def _ffn_down_body(h_ref, w_ref, g_ref, x_ref, o_ref):
    y = _dot(h_ref[...], w_ref[...])
    o_ref[...] = x_ref[...] + _rms(y) * g_ref[...]


def _ffn_down(h, w, layer, g, x2):
    m, d = x2.shape
    f = h.shape[1]
    tm = min(256, m)
    return pl.pallas_call(
        _ffn_down_body,
        out_shape=jax.ShapeDtypeStruct((m, d), F32),
        grid=(m // tm,),
        in_specs=[pl.BlockSpec((tm, f), lambda i: (i, 0)),
                  pl.BlockSpec((None, f, d), lambda i: (layer, 0, 0), pipeline_mode=pl.Buffered(1)),
                  pl.BlockSpec((1, d), lambda i: (0, 0)),
                  pl.BlockSpec((tm, d), lambda i: (i, 0))],
        out_specs=pl.BlockSpec((tm, d), lambda i: (i, 0)),
        compiler_params=_cparams(1),
        name="ffn_down",
    )(h, w, g, x2)


def _pack_w_in(w):
    offs = np.cumsum([0] + IN_SIZES)
    seg = lambda k: w[:, offs[k]:offs[k + 1]]
    small = jnp.concatenate(
        [seg(3), seg(11),
         jnp.zeros((w.shape[0], NP_COLS - CB_SM * LANES - FOX_HEADS - 3 * NSA_HEADS), w.dtype)], axis=1)
    cols = [w[:, offs[12]:offs[14]], w[:, offs[0]:offs[3]], w[:, offs[4]:offs[11]], small]
    return jnp.concatenate(cols, axis=1).astype(MXU_DT)


def _rope_tables(pos):
    inv_freq = jnp.power(ROPE_THETA, -jnp.arange(ROPE_HALF, dtype=F32) * (2.0 / ROPE_DIM))
    ang = pos[:, None] * inv_freq[None, :]
    cos, sin = jnp.cos(ang), jnp.sin(ang)
    n = pos.shape[0]
    pad = HEAD_DIM - ROPE_DIM
    c = jnp.concatenate([cos, cos, jnp.ones((n, pad), F32)], axis=1)
    s = jnp.concatenate([-sin, sin, jnp.zeros((n, pad), F32)], axis=1)
    return c, s


def _overlap_t(T):
    nb, ncp = T // SLC_LEN, T // CMP_STRIDE
    sc = np.arange(ncp) * CMP_STRIDE
    ss = np.arange(nb) * SLC_LEN
    ov = np.minimum(sc[None, :] + CMP_LEN, ss[:, None] + SLC_LEN) - np.maximum(sc[None, :], ss[:, None])
    ov = np.clip(ov, 0, None) / CMP_LEN
    ov[:, ncp - 1] = 0.0
    return jnp.asarray(ov, dtype=MXU_DT)


def _layer(x2, B, T, layer, n_mix_pre, n_mix_post, n_ffn_pre, n_ffn_post, w_in, f_bias,
           ck_pe, ck_w1, ck_w2, cv_pe, cv_w1, cv_w2, stacked, tables):
    ct, st, cc, sc, ovt = tables
    w_up_fox, w_up_nsa, w_out, w_gate, w_up, w_down = stacked
    row = lambda v: v.reshape(1, -1).astype(F32)
    bf = lambda w: w.astype(MXU_DT)
    proj = _inproj(x2, row(n_mix_pre), _pack_w_in(w_in))
    bias_row = jnp.pad(f_bias.astype(F32), (0, LANES - FOX_HEADS)).reshape(1, LANES)
    cum = _foxcum(proj, bias_row, B, T)
    o_fox = _fox(proj, cum, B, T)
    kc, vc = _compress(proj, ck_pe, bf(ck_w1), bf(ck_w2), cv_pe, bf(cv_w1), bf(cv_w2), cc, sc, B, T)
    o_nsa = _nsa(proj, kc, vc, ct, st, ovt, B, T)
    mix = _merge(o_fox, o_nsa, w_up_fox, w_up_nsa, layer, proj)
    x2 = _outproj(mix, w_out, layer, row(n_mix_post), x2)
    h = _ffn_up(x2, row(n_ffn_pre), w_gate, w_up, layer)
    return _ffn_down(h, w_down, layer, row(n_ffn_post), x2)


@jax.jit
def kernel(x, norm_mix_pre, norm_mix_post, norm_ffn_pre, norm_ffn_post, w_in, fox_forget_bias, cmp_k_pe, cmp_k_w1, cmp_k_w2, cmp_v_pe, cmp_v_w1, cmp_v_w2, w_up_fox, w_up_nsa, w_out, w_ffn_gate, w_ffn_up, w_ffn_down):
    B, T, D = x.shape
    ct, st = _rope_tables(jnp.arange(T, dtype=F32))
    ncp = T // CMP_STRIDE
    cc, sc = _rope_tables((jnp.arange(ncp) * CMP_STRIDE + CMP_LEN - 1).astype(F32))
    tables = (ct, st, cc, sc, _overlap_t(T))
    x2 = x.reshape(B * T, D)
    stacked = tuple(w.astype(MXU_DT)
                    for w in (w_up_fox, w_up_nsa, w_out, w_ffn_gate, w_ffn_up, w_ffn_down))
    for l in range(w_in.shape[0]):
        x2 = _layer(x2, B, T, l, norm_mix_pre[l], norm_mix_post[l], norm_ffn_pre[l], norm_ffn_post[l],
                    w_in[l], fox_forget_bias[l],
                    cmp_k_pe[l], cmp_k_w1[l], cmp_k_w2[l], cmp_v_pe[l], cmp_v_w1[l], cmp_v_w2[l],
                    stacked, tables)
    return x2.reshape(B, T, D)
```

```python
import functools

import numpy as np
import jax
import jax.numpy as jnp
from jax import lax
from jax.experimental import pallas as pl
from jax.experimental.pallas import tpu as pltpu

D_MODEL = 2048
HEAD_DIM = 128
FOX_HEADS = 8
NSA_HEADS = 8
NSA_GROUP_SIZE = 4
NSA_KV_GROUPS = 2
FOX_W = FOX_HEADS * HEAD_DIM
NSA_W = NSA_HEADS * HEAD_DIM
NSA_KV_W = NSA_KV_GROUPS * HEAD_DIM
ROPE_DIM = HEAD_DIM // 4
ROPE_HALF = ROPE_DIM // 2
ROPE_THETA = 500000.0
CMP_LEN = 32
CMP_STRIDE = 16
CMP_HIDDEN = 256
SLC_LEN = 64
SLC_SHIFT = SLC_LEN.bit_length() - 1
CMP_SHIFT = CMP_STRIDE.bit_length() - 1
SLC_TOP = 16
WINDOW = 512
FFN_HIDDEN = 5632
EPS = 1e-6
NEG_INF = -1e30
FORCE_SCORE = 1e6
SCALE = HEAD_DIM ** -0.5
LOG2E = float(np.log2(np.e))
QSCALE = SCALE * LOG2E
IN_SIZES = [FOX_W, FOX_W, FOX_W, FOX_HEADS, NSA_W,
            NSA_KV_W, NSA_KV_W, NSA_KV_W, NSA_KV_W, NSA_KV_W, NSA_KV_W,
            NSA_HEADS * 3, D_MODEL, D_MODEL]

LANES = 128
MXU_DT = jnp.bfloat16
VMEM_LIMIT = 56 * 1024 * 1024

CB_GF, CB_GN = 0, 16
CB_FQ, CB_FK, CB_FV, CB_NQ = 32, 40, 48, 56
CB_NKC, CB_NVC, CB_NKS, CB_NVS, CB_NKW, CB_NVW = 64, 66, 68, 70, 72, 74
CB_SM = 76
NP_COLS = 78 * LANES
SM_FF, SM_GATE = 0, FOX_HEADS
MXU_N = 256

F32 = jnp.float32


def _cparams(n_axes):
    return pltpu.CompilerParams(dimension_semantics=("arbitrary",) * n_axes,
                                vmem_limit_bytes=VMEM_LIMIT)


def _rms(x):
    return x * lax.rsqrt(jnp.mean(x * x, axis=-1, keepdims=True) + EPS)


def _dot(a, b):
    return jnp.dot(a, b, preferred_element_type=F32)


def _dot_nt(a, b):
    return lax.dot_general(a, b, (((1,), (1,)), ((), ())), preferred_element_type=F32)


def _split3(x):
    hi = x.astype(MXU_DT)
    r1 = x - hi.astype(F32)
    mid = r1.astype(MXU_DT)
    lo = (r1 - mid.astype(F32)).astype(MXU_DT)
    return hi, mid, lo


def _rope(x, c, s):
    lane = lax.broadcasted_iota(jnp.int32, x.shape, 1)
    partner = jnp.where(lane < ROPE_HALF,
                        pltpu.roll(x, LANES - ROPE_HALF, 1),
                        pltpu.roll(x, ROPE_HALF, 1))
    return x * c + partner * s


def _sigmoid(z):
    return 1.0 / (1.0 + jnp.exp(-z))


def _inproj_body(x_ref, g_ref, w_ref, o_ref, a_ref):
    @pl.when(pl.program_id(1) == 0)
    def _():
        a_ref[...] = (_rms(x_ref[...]) * g_ref[...]).astype(a_ref.dtype)

    o_ref[...] = _dot(a_ref[...], w_ref[...])


def _inproj(x2, g, w):
    m, d = x2.shape
    n = w.shape[1]
    tm = min(1024, m)
    tn = 13 * LANES
    return pl.pallas_call(
        _inproj_body,
        out_shape=jax.ShapeDtypeStruct((m, n), F32),
        grid=(m // tm, n // tn),
        in_specs=[pl.BlockSpec((tm, d), lambda i, j: (i, 0)),
                  pl.BlockSpec((1, d), lambda i, j: (0, 0)),
                  pl.BlockSpec((d, tn), lambda i, j: (0, j))],
        out_specs=pl.BlockSpec((tm, tn), lambda i, j: (i, j)),
        scratch_shapes=[pltpu.VMEM((tm, d), MXU_DT)],
        compiler_params=_cparams(2),
        name="inproj",
    )(x2, g, w)


def _foxcum_body(s_ref, b_ref, cum_ref, *, T, CH):
    r = lax.broadcasted_iota(jnp.int32, (CH, CH), 0)
    c = lax.broadcasted_iota(jnp.int32, (CH, CH), 1)
    tri = jnp.where(c <= r, 1.0, 0.0).astype(MXU_DT)
    carry = jnp.zeros((1, LANES), F32)
    for ci in range(T // CH):
        z = s_ref[ci * CH:(ci + 1) * CH, :] + b_ref[...]
        lf = jnp.minimum(z, 0.0) - jnp.log1p(jnp.exp(-jnp.abs(z)))
        hi, mid, lo = _split3(lf)
        out = (_dot(tri, hi) + _dot(tri, mid)) + _dot(tri, lo) + carry
        cum_ref[ci * CH:(ci + 1) * CH, :] = out
        carry = out[CH - 1:CH, :]


def _foxcum(proj, bias_row, B, T):
    CH = 256
    return pl.pallas_call(
        functools.partial(_foxcum_body, T=T, CH=CH),
        out_shape=jax.ShapeDtypeStruct((B * T, LANES), F32),
        grid=(B,),
        in_specs=[pl.BlockSpec((T, LANES), lambda b: (b, CB_SM)),
                  pl.BlockSpec((1, LANES), lambda b: (0, 0))],
        out_specs=pl.BlockSpec((T, LANES), lambda b: (b, 0)),
        compiler_params=_cparams(1),
        name="foxcum",
    )(proj, bias_row)


def _online_softmax_step_t(st, vt_t, m_ref, l_ref, acc_ref):
    m_old = m_ref[...]
    m_new = jnp.maximum(m_old, jnp.max(st, axis=0, keepdims=True))
    alpha = jnp.exp2(m_old - m_new)
    p = jnp.exp2(st - m_new)
    l_ref[...] = alpha * l_ref[...] + jnp.sum(p, axis=0, keepdims=True)
    acc_ref[...] = alpha * acc_ref[...] + _dot(vt_t, p.astype(MXU_DT))
    m_ref[...] = m_new


def _fox_body(q_ref, k_ref, v_ref, cum_ref, o_ref,
              ka_ref, vt_ref, m_ref, l_ref, acc_ref, sa_ref, sb_ref, *, T, TQ, NH):
    hp = pl.program_id(1)
    i = pl.program_id(2)
    RC = 512
    heads = range(NH)

    def cum_pieces(hh, rows, n):
        lane = lax.broadcasted_iota(jnp.int32, (n, LANES), 1)
        col = jnp.sum(jnp.where(lane == hp * NH + hh, cum_ref[rows, :], 0.0), axis=1, keepdims=True)
        return [p.astype(F32) for p in _split3(col * LOG2E)]

    @pl.when(i == 0)
    def _():
        for hh in heads:
            hcols = slice(hh * HEAD_DIM, (hh + 1) * HEAD_DIM)
            for c0 in range(0, T, RC):
                sl = slice(c0, c0 + RC)
                ka_ref[hh, sl, 0:HEAD_DIM] = k_ref[sl, hcols].astype(MXU_DT)
                lane = lax.broadcasted_iota(jnp.int32, (RC, LANES), 1)
                aug = jnp.where((lane >= 3) & (lane < 6), 1.0, 0.0)
                for idx, c in enumerate(cum_pieces(hh, sl, RC)):
                    aug = jnp.where(lane == idx, c, aug)
                ka_ref[hh, sl, HEAD_DIM:HEAD_DIM + LANES] = aug.astype(MXU_DT)
                vt_ref[hh, :, sl] = v_ref[sl, hcols].T.astype(MXU_DT)

    q0 = pl.multiple_of(i * TQ, TQ)
    lane = lax.broadcasted_iota(jnp.int32, (TQ, LANES), 1)
    qa = []
    for hh in heads:
        qaug = jnp.where(lane < 3, -1.0, 0.0)
        for idx, c in enumerate(cum_pieces(hh, pl.ds(q0, TQ), TQ)):
            qaug = jnp.where(lane == 3 + idx, c, qaug)
        qh = q_ref[:, hh * HEAD_DIM:(hh + 1) * HEAD_DIM] * QSCALE
        qa.append(jnp.concatenate([qh.astype(MXU_DT), qaug.astype(MXU_DT)], axis=1))
    m_ref[...] = jnp.full(m_ref.shape, NEG_INF, F32)
    l_ref[...] = jnp.zeros(l_ref.shape, F32)
    acc_ref[...] = jnp.zeros(acc_ref.shape, F32)

    def scores(t, s_ref):
        k0 = pl.multiple_of(t * TQ, TQ)
        for hh in heads:
            s_ref[hh] = _dot_nt(ka_ref[hh, pl.ds(k0, TQ), :], qa[hh])

    def consume(t, s_ref, diagonal):
        k0 = pl.multiple_of(t * TQ, TQ)
        for hh in heads:
            st = s_ref[hh]
            if diagonal:
                krow = lax.broadcasted_iota(jnp.int32, (TQ, TQ), 0)
                qcol = lax.broadcasted_iota(jnp.int32, (TQ, TQ), 1)
                st = jnp.where(krow <= qcol, st, NEG_INF)
            _online_softmax_step_t(st, vt_ref[hh, :, pl.ds(k0, TQ)],
                                   m_ref.at[hh], l_ref.at[hh], acc_ref.at[hh])

    scores(0, sa_ref)

    def pair(jj, carry):
        scores(2 * jj + 1, sb_ref)
        consume(2 * jj, sa_ref, False)
        scores(2 * jj + 2, sa_ref)
        consume(2 * jj + 1, sb_ref, False)
        return carry

    lax.fori_loop(0, i // 2, pair, 0)

    @pl.when(i % 2 == 0)
    def _():
        consume(i, sa_ref, True)

    @pl.when(i % 2 == 1)
    def _():
        scores(i, sb_ref)
        consume(i - 1, sa_ref, False)
        consume(i, sb_ref, True)

    for hh in heads:
        o_ref[:, hh * HEAD_DIM:(hh + 1) * HEAD_DIM] = (acc_ref[hh] / l_ref[hh]).T.astype(o_ref.dtype)


def _fox(proj, cum, B, T):
    TQ = 512
    NH = 4
    nq = T // TQ
    hw = NH * HEAD_DIM
    one = pl.Buffered(1)
    return pl.pallas_call(
        functools.partial(_fox_body, T=T, TQ=TQ, NH=NH),
        out_shape=jax.ShapeDtypeStruct((B * T, FOX_W), MXU_DT),
        grid=(B, FOX_HEADS // NH, nq),
        in_specs=[pl.BlockSpec((TQ, hw), lambda b, h, i: (b * nq + i, CB_FQ // NH + h)),
                  pl.BlockSpec((T, hw), lambda b, h, i: (b, CB_FK // NH + h), pipeline_mode=one),
                  pl.BlockSpec((T, hw), lambda b, h, i: (b, CB_FV // NH + h), pipeline_mode=one),
                  pl.BlockSpec((T, LANES), lambda b, h, i: (b, 0), pipeline_mode=one)],
        out_specs=pl.BlockSpec((TQ, hw), lambda b, h, i: (b * nq + i, h)),
        scratch_shapes=[pltpu.VMEM((NH, T, HEAD_DIM + LANES), MXU_DT),
                        pltpu.VMEM((NH, HEAD_DIM, T), MXU_DT),
                        pltpu.VMEM((NH, 1, TQ), F32),
                        pltpu.VMEM((NH, 1, TQ), F32),
                        pltpu.VMEM((NH, HEAD_DIM, TQ), F32),
                        pltpu.VMEM((NH, TQ, TQ), F32),
                        pltpu.VMEM((NH, TQ, TQ), F32)],
        compiler_params=_cparams(3),
        name="fox",
    )(proj, proj, proj, cum)


def _gelu_tanh(x):
    return 0.5 * x * (1.0 + jnp.tanh(np.sqrt(2.0 / np.pi).astype(np.float32)
                                     * (x + 0.044715 * (x * x * x))))


def _compress_one(x_ref, pe_ref, w1_ref, w2_ref, NCP):
    h0 = jnp.zeros((NCP, CMP_HIDDEN), F32)
    h1 = jnp.zeros((NCP, CMP_HIDDEN), F32)
    for l in range(CMP_STRIDE):
        xl = x_ref[pl.ds(l, NCP, stride=CMP_STRIDE), :]
        a0 = (xl + pe_ref[l:l + 1, :]).astype(MXU_DT)
        a1 = (xl + pe_ref[CMP_STRIDE + l:CMP_STRIDE + l + 1, :]).astype(MXU_DT)
        h0 = h0 + _dot(a0, w1_ref[l * HEAD_DIM:(l + 1) * HEAD_DIM, :])
        h1 = h1 + _dot(a1, w1_ref[(CMP_STRIDE + l) * HEAD_DIM:(CMP_STRIDE + l + 1) * HEAD_DIM, :])
    hsum = h0 + pltpu.roll(h1, NCP - 1, 0)
    return _dot(_gelu_tanh(hsum).astype(MXU_DT), w2_ref[...])


def _compress_body(xk_ref, xv_ref, pek_ref, w1k_ref, w2k_ref, pev_ref, w1v_ref, w2v_ref,
                   cc_ref, sc_ref, kc_ref, vc_ref, *, NCP):
    kc = _compress_one(xk_ref, pek_ref, w1k_ref, w2k_ref, NCP)
    kc_ref[0, 0] = _rope(kc, cc_ref[...], sc_ref[...]).astype(kc_ref.dtype)
    vc = _compress_one(xv_ref, pev_ref, w1v_ref, w2v_ref, NCP)
    vc_ref[0, 0] = vc.T.astype(vc_ref.dtype)


def _compress(proj, pek, w1k, w2k, pev, w1v, w2v, cc, sc, B, T):
    NCP = T // CMP_STRIDE
    full = lambda shape: pl.BlockSpec(shape, lambda b, g: (0,) * len(shape))
    k_sds = jax.ShapeDtypeStruct((B, NSA_KV_GROUPS, NCP, HEAD_DIM), MXU_DT)
    v_sds = jax.ShapeDtypeStruct((B, NSA_KV_GROUPS, HEAD_DIM, NCP), MXU_DT)
    k_spec = pl.BlockSpec((1, 1, NCP, HEAD_DIM), lambda b, g: (b, g, 0, 0))
    v_spec = pl.BlockSpec((1, 1, HEAD_DIM, NCP), lambda b, g: (b, g, 0, 0))
    return pl.pallas_call(
        functools.partial(_compress_body, NCP=NCP),
        out_shape=(k_sds, v_sds),
        grid=(B, NSA_KV_GROUPS),
        in_specs=[pl.BlockSpec((T, LANES), lambda b, g: (b, CB_NKC + g)),
                  pl.BlockSpec((T, LANES), lambda b, g: (b, CB_NVC + g)),
                  full((CMP_LEN, HEAD_DIM)), full((CMP_LEN * HEAD_DIM, CMP_HIDDEN)),
                  full((CMP_HIDDEN, HEAD_DIM)),
                  full((CMP_LEN, HEAD_DIM)), full((CMP_LEN * HEAD_DIM, CMP_HIDDEN)),
                  full((CMP_HIDDEN, HEAD_DIM)),
                  full((NCP, LANES)), full((NCP, LANES))],
        out_specs=(k_spec, v_spec),
        compiler_params=_cparams(2),
        name="compress",
    )(proj, proj, pek, w1k, w2k, pev, w1v, w2v, cc, sc)


def _nsa_body(q_ref, sm_ref, cq_ref, sq_ref, kc_ref, vc_ref, ks_ref, vs_ref, kw_ref, vw_ref,
              ct_ref, st_ref, ovt_ref, o_ref,
              ksa_ref, vst_ref, kwb_ref, vwt_ref, smt_ref, m_ref, l_ref, acc_ref, sa_ref, sb_ref,
              part_ref, *, T, TQ, TK, WSP, NB, NCP, NSEL):
    i = pl.program_id(1)
    HG = NSA_GROUP_SIZE
    NG = NSA_KV_GROUPS
    groups = range(NG)
    R = HG * TQ
    RC = 512

    @pl.when(i == 0)
    def _():
        for c0 in range(0, T, RC):
            sl = slice(c0, c0 + RC)
            ct = ct_ref[sl, :]
            st = st_ref[sl, :]
            kblk = (c0 + lax.broadcasted_iota(jnp.int32, (RC, LANES), 0)) >> SLC_SHIFT
            onehot = jnp.where(kblk == lax.broadcasted_iota(jnp.int32, (RC, LANES), 1), 1.0, 0.0)
            for gg in groups:
                gc = slice(gg * HEAD_DIM, (gg + 1) * HEAD_DIM)
                ksa_ref[gg, sl, 0:HEAD_DIM] = _rope(ks_ref[sl, gc], ct, st).astype(MXU_DT)
                ksa_ref[gg, sl, HEAD_DIM:HEAD_DIM + LANES] = onehot.astype(MXU_DT)
                kwb_ref[gg, sl, :] = _rope(kw_ref[sl, gc], ct, st).astype(MXU_DT)
                vst_ref[gg, :, sl] = vs_ref[sl, gc].T.astype(MXU_DT)
                vwt_ref[gg, :, sl] = vw_ref[sl, gc].T.astype(MXU_DT)

    q0 = i * TQ
    cq = cq_ref[...]
    sq = sq_ref[...]
    qpos = q0 + (lax.broadcasted_iota(jnp.int32, (1, R), 1) & (TQ - 1))
    smt_ref[...] = sm_ref[...].T

    def gate_row(gg, c):
        return jnp.concatenate(
            [_sigmoid(smt_ref[SM_GATE + 3 * (gg * HG + hh) + c:SM_GATE + 3 * (gg * HG + hh) + c + 1, :])
             for hh in range(HG)], axis=1)

    w0 = pl.multiple_of(jnp.clip((i + 1) * TQ - WSP, 0, T - WSP), LANES)
    dist = (q0 + lax.broadcasted_iota(jnp.int32, (WSP, TQ), 1)) \
        - (w0 + lax.broadcasted_iota(jnp.int32, (WSP, TQ), 0))
    wbias = jnp.where((dist >= 0) & (dist < WINDOW), 0.0, NEG_INF)
    wbias = jnp.concatenate([wbias] * HG, axis=1)
    ovt = ovt_ref[...]
    jblk = lax.broadcasted_iota(jnp.int32, (NB, TQ), 0)
    cur = (q0 + lax.broadcasted_iota(jnp.int32, (NB, TQ), 1)) >> SLC_SHIFT
    forced = (jblk == 0) | (jblk == cur) | (jblk == cur - 1)
    future = jblk > cur
    SUB = 8
    sub = lax.broadcasted_iota(jnp.int32, (SUB, TQ), 0)

    qa = []
    for gg in groups:
        qb = jnp.concatenate(
            [_rope(q_ref[:, (gg * HG + hh) * HEAD_DIM:(gg * HG + hh + 1) * HEAD_DIM], cq, sq) * QSCALE
             for hh in range(HG)], axis=0).astype(MXU_DT)

        sc = _dot_nt(kc_ref[0, gg], qb)
        nrow = lax.broadcasted_iota(jnp.int32, (NCP, R), 0)
        last_valid = (qpos - (CMP_LEN - 1)) >> CMP_SHIFT
        sc = jnp.where(nrow <= last_valid, sc, NEG_INF)
        pc = jnp.exp2(sc - jnp.max(sc, axis=0, keepdims=True))
        pc = pc * (jnp.where(qpos >= CMP_LEN - 1, 1.0, 0.0) / jnp.sum(pc, axis=0, keepdims=True))
        o_c = _dot(vc_ref[0, gg], pc.astype(MXU_DT))

        sw = _dot_nt(kwb_ref[gg, pl.ds(w0, WSP), :], qb) + wbias
        pw = jnp.exp2(sw - jnp.max(sw, axis=0, keepdims=True))
        o_w = _dot(vwt_ref[gg, :, pl.ds(w0, WSP)], pw.astype(MXU_DT)) / jnp.sum(pw, axis=0, keepdims=True)
        part_ref[gg] = gate_row(gg, 0) * o_c + gate_row(gg, 2) * o_w

        psum = (pc[:, 0:TQ] + pc[:, TQ:2 * TQ]) + (pc[:, 2 * TQ:3 * TQ] + pc[:, 3 * TQ:4 * TQ])
        hi, mid, lo = _split3(psum)
        imp = (_dot(ovt, hi) + _dot(ovt, mid)) + _dot(ovt, lo)
        imp = jnp.where(forced, FORCE_SCORE, jnp.where(future, -1.0, imp))
        blocks = [imp[r * SUB:(r + 1) * SUB, :] for r in range(NB // SUB)]
        ranks = [jnp.zeros((SUB, TQ), F32) for _ in blocks]
        for jp in range(NB):
            gj, sj = divmod(jp, SUB)
            row = blocks[gj][sj:sj + 1, :]
            for r, blk in enumerate(blocks):
                if r > gj:
                    beats = jnp.where(row >= blk, 1.0, 0.0)
                elif r < gj:
                    beats = jnp.where(row > blk, 1.0, 0.0)
                else:
                    beats = jnp.where(sub > sj, jnp.where(row >= blk, 1.0, 0.0),
                                      jnp.where(row > blk, 1.0, 0.0))
                ranks[r] = ranks[r] + beats
        rank = jnp.concatenate(ranks, axis=0)
        sel_t = jnp.where((rank < NSEL) & (imp >= 0.0), 1.0, 0.0)
        if NB < LANES:
            sel_t = jnp.concatenate([sel_t, jnp.zeros((LANES - NB, TQ), F32)], axis=0)
        selneg = jnp.where(sel_t.T > 0.5, 0.0, NEG_INF).astype(MXU_DT)
        qa.append(jnp.concatenate([qb, jnp.concatenate([selneg] * HG, axis=0)], axis=1))

    m_ref[...] = jnp.full(m_ref.shape, NEG_INF, F32)
    l_ref[...] = jnp.zeros(l_ref.shape, F32)
    acc_ref[...] = jnp.zeros(acc_ref.shape, F32)

    def scores(t, s_ref):
        k0 = pl.multiple_of(t * TK, TK)
        for gg in groups:
            s_ref[gg] = _dot_nt(ksa_ref[gg, pl.ds(k0, TK), :], qa[gg])

    def consume(t, s_ref, causal):
        k0 = pl.multiple_of(t * TK, TK)
        for gg in groups:
            if causal:
                own = pl.ds(pl.multiple_of(q0 - k0, TQ), TQ)
                krow = lax.broadcasted_iota(jnp.int32, (TQ, TQ), 0)
                qcol = lax.broadcasted_iota(jnp.int32, (TQ, TQ), 1)
                tri = jnp.where(krow <= qcol, 0.0, NEG_INF)
                s_ref[gg, own, :] = s_ref[gg, own, :] + jnp.concatenate([tri] * HG, axis=1)
            _online_softmax_step_t(s_ref[gg], vst_ref[gg, :, pl.ds(k0, TK)],
                                   m_ref.at[gg], l_ref.at[gg], acc_ref.at[gg])

    last = ((i + 1) * TQ + TK - 1) // TK - 1
    scores(0, sa_ref)

    def pair(jj, carry):
        scores(2 * jj + 1, sb_ref)
        consume(2 * jj, sa_ref, False)
        scores(2 * jj + 2, sa_ref)
        consume(2 * jj + 1, sb_ref, False)
        return carry

    lax.fori_loop(0, last // 2, pair, 0)

    @pl.when(last % 2 == 0)
    def _():
        consume(last, sa_ref, True)

    @pl.when(last % 2 == 1)
    def _():
        scores(last, sb_ref)
        consume(last - 1, sa_ref, False)
        consume(last, sb_ref, True)

    for gg in groups:
        o = part_ref[gg] + (gate_row(gg, 1) / l_ref[gg]) * acc_ref[gg]
        for hh in range(HG):
            oc = (gg * HG + hh) * HEAD_DIM
            o_ref[:, oc:oc + HEAD_DIM] = o[:, hh * TQ:(hh + 1) * TQ].T.astype(o_ref.dtype)


def _nsa(proj, kc, vc, ct, st, ovt, B, T):
    TQ = 128
    TK = 512
    WSP = WINDOW + TQ
    NB = T // SLC_LEN
    NCP = T // CMP_STRIDE
    nq = T // TQ
    HG = NSA_GROUP_SIZE
    NG = NSA_KV_GROUPS
    R = HG * TQ
    one = pl.Buffered(1)
    kv_spec = lambda cb: pl.BlockSpec((T, NG * LANES), lambda b, i: (b, cb // NG), pipeline_mode=one)
    return pl.pallas_call(
        functools.partial(_nsa_body, T=T, TQ=TQ, TK=TK, WSP=WSP, NB=NB, NCP=NCP,
                          NSEL=min(SLC_TOP, NB)),
        out_shape=jax.ShapeDtypeStruct((B * T, NSA_W), MXU_DT),
        grid=(B, nq),
        in_specs=[pl.BlockSpec((TQ, NSA_W), lambda b, i: (b * nq + i, CB_NQ * LANES // NSA_W)),
                  pl.BlockSpec((TQ, LANES), lambda b, i: (b * nq + i, CB_SM)),
                  pl.BlockSpec((TQ, LANES), lambda b, i: (i, 0)),
                  pl.BlockSpec((TQ, LANES), lambda b, i: (i, 0)),
                  pl.BlockSpec((1, NG, NCP, HEAD_DIM), lambda b, i: (b, 0, 0, 0)),
                  pl.BlockSpec((1, NG, HEAD_DIM, NCP), lambda b, i: (b, 0, 0, 0)),
                  kv_spec(CB_NKS), kv_spec(CB_NVS), kv_spec(CB_NKW), kv_spec(CB_NVW),
                  pl.BlockSpec((T, LANES), lambda b, i: (0, 0), pipeline_mode=one),
                  pl.BlockSpec((T, LANES), lambda b, i: (0, 0), pipeline_mode=one),
                  pl.BlockSpec((NB, NCP), lambda b, i: (0, 0))],
        out_specs=pl.BlockSpec((TQ, NSA_W), lambda b, i: (b * nq + i, 0)),
        scratch_shapes=[pltpu.VMEM((NG, T, HEAD_DIM + LANES), MXU_DT),
                        pltpu.VMEM((NG, HEAD_DIM, T), MXU_DT),
                        pltpu.VMEM((NG, T, LANES), MXU_DT),
                        pltpu.VMEM((NG, HEAD_DIM, T), MXU_DT),
                        pltpu.VMEM((LANES, TQ), F32),
                        pltpu.VMEM((NG, 1, R), F32),
                        pltpu.VMEM((NG, 1, R), F32),
                        pltpu.VMEM((NG, HEAD_DIM, R), F32),
                        pltpu.VMEM((NG, TK, R), F32),
                        pltpu.VMEM((NG, TK, R), F32),
                        pltpu.VMEM((NG, HEAD_DIM, R), F32)],
        compiler_params=_cparams(2),
        name="nsa",
    )(proj, proj, ct, st, kc, vc, proj, proj, proj, proj, ct, st, ovt)


def _merge_body(of_ref, on_ref, wf_ref, wn_ref, gf_ref, gn_ref, o_ref, *, CN):
    of = of_ref[...]
    on = on_ref[...]
    for c0 in range(0, D_MODEL, CN):
        cols = slice(c0, c0 + CN)
        yf = _dot(of, wf_ref[:, cols])
        yn = _dot(on, wn_ref[:, cols])
        o_ref[:, cols] = (_sigmoid(gf_ref[:, cols]) * yf + _sigmoid(gn_ref[:, cols]) * yn).astype(o_ref.dtype)


def _merge(o_fox, o_nsa, wf, wn, layer, proj):
    m = o_fox.shape[0]
    tm = min(512, m)
    d = D_MODEL
    one = pl.Buffered(1)
    return pl.pallas_call(
        functools.partial(_merge_body, CN=2 * MXU_N),
        out_shape=jax.ShapeDtypeStruct((m, d), MXU_DT),
        grid=(m // tm,),
        in_specs=[pl.BlockSpec((tm, FOX_W), lambda i: (i, 0)),
                  pl.BlockSpec((tm, NSA_W), lambda i: (i, 0)),
                  pl.BlockSpec((None, FOX_W, d), lambda i: (layer, 0, 0), pipeline_mode=one),
                  pl.BlockSpec((None, NSA_W, d), lambda i: (layer, 0, 0), pipeline_mode=one),
                  pl.BlockSpec((tm, d), lambda i: (i, CB_GF * LANES // d)),
                  pl.BlockSpec((tm, d), lambda i: (i, CB_GN * LANES // d))],
        out_specs=pl.BlockSpec((tm, d), lambda i: (i, 0)),
        compiler_params=_cparams(1),
        name="merge",
    )(o_fox, o_nsa, wf, wn, proj, proj)


def _outproj_body(a_ref, w_ref, g_ref, x_ref, o_ref):
    y = _dot(a_ref[...], w_ref[...])
    o_ref[...] = x_ref[...] + _rms(y) * g_ref[...]


def _outproj(mix, w, layer, g, x2):
    m, d = x2.shape
    tm = min(512, m)
    return pl.pallas_call(
        _outproj_body,
        out_shape=jax.ShapeDtypeStruct((m, d), F32),
        grid=(m // tm,),
        in_specs=[pl.BlockSpec((tm, d), lambda i: (i, 0)),
                  pl.BlockSpec((None, d, d), lambda i: (layer, 0, 0)),
                  pl.BlockSpec((1, d), lambda i: (0, 0)),
                  pl.BlockSpec((tm, d), lambda i: (i, 0))],
        out_specs=pl.BlockSpec((tm, d), lambda i: (i, 0)),
        compiler_params=_cparams(1),
        name="outproj",
    )(mix, w, g, x2)


def _ffn_up_body(x_ref, g_ref, wg_ref, wu_ref, o_ref, a_ref):
    @pl.when(pl.program_id(1) == 0)
    def _():
        a_ref[...] = (_rms(x_ref[...]) * g_ref[...]).astype(a_ref.dtype)

    a = a_ref[...]
    hg = _dot(a, wg_ref[...].astype(MXU_DT))
    hu = _dot(a, wu_ref[...].astype(MXU_DT))
    o_ref[...] = (hg * _sigmoid(hg) * hu).astype(o_ref.dtype)


def _ffn_up(x2, g, wg, wu, layer):
    m, d = x2.shape
    f = wg.shape[2]
    tm = min(1024, m)
    tn = 512
    return pl.pallas_call(
        _ffn_up_body,
        out_shape=jax.ShapeDtypeStruct((m, f), MXU_DT),
        grid=(m // tm, f // tn),
        in_specs=[pl.BlockSpec((tm, d), lambda i, j: (i, 0)),
                  pl.BlockSpec((1, d), lambda i, j: (0, 0)),
                  pl.BlockSpec((None, d, tn), lambda i, j: (layer, 0, j)),
                  pl.BlockSpec((None, d, tn), lambda i, j: (layer, 0, j))],
        out_specs=pl.BlockSpec((tm, tn), lambda i, j: (i, j)),
        scratch_shapes=[pltpu.VMEM((tm, d), MXU_DT)],
        compiler_params=_cparams(2),
        name="ffn_up",
    )(x2, g, wg, wu)


def _ffn_down_body(h_ref, w_ref, g_ref, x_ref, o_ref):
    y = _dot(h_ref[...], w_ref[...])
    o_ref[...] = x_ref[...] + _rms(y) * g_ref[...]


def _ffn_down(h, w, layer, g, x2):
    m, d = x2.shape
    f = h.shape[1]
    tm = min(256, m)
    return pl.pallas_call(
        _ffn_down_body,
        out_shape=jax.ShapeDtypeStruct((m, d), F32),
        grid=(m // tm,),
        in_specs=[pl.BlockSpec((tm, f), lambda i: (i, 0)),
                  pl.BlockSpec((None, f, d), lambda i: (layer, 0, 0), pipeline_mode=pl.Buffered(1)),
                  pl.BlockSpec((1, d), lambda i: (0, 0)),
                  pl.BlockSpec((tm, d), lambda i: (i, 0))],
        out_specs=pl.BlockSpec((tm, d), lambda i: (i, 0)),
        compiler_params=_cparams(1),
        name="ffn_down",
    )(h, w, g, x2)


def _pack_w_in(w):
    offs = np.cumsum([0] + IN_SIZES)
    seg = lambda k: w[:, offs[k]:offs[k + 1]]
    small = jnp.concatenate(
        [seg(3), seg(11),
         jnp.zeros((w.shape[0], NP_COLS - CB_SM * LANES - FOX_HEADS - 3 * NSA_HEADS), w.dtype)], axis=1)
    cols = [w[:, offs[12]:offs[14]], w[:, offs[0]:offs[3]], w[:, offs[4]:offs[11]], small]
    return jnp.concatenate(cols, axis=1).astype(MXU_DT)


def _rope_tables(pos):
    inv_freq = jnp.power(ROPE_THETA, -jnp.arange(ROPE_HALF, dtype=F32) * (2.0 / ROPE_DIM))
    ang = pos[:, None] * inv_freq[None, :]
    cos, sin = jnp.cos(ang), jnp.sin(ang)
    n = pos.shape[0]
    pad = HEAD_DIM - ROPE_DIM
    c = jnp.concatenate([cos, cos, jnp.ones((n, pad), F32)], axis=1)
    s = jnp.concatenate([-sin, sin, jnp.zeros((n, pad), F32)], axis=1)
    return c, s


def _overlap_t(T):
    nb, ncp = T // SLC_LEN, T // CMP_STRIDE
    sc = np.arange(ncp) * CMP_STRIDE
    ss = np.arange(nb) * SLC_LEN
    ov = np.minimum(sc[None, :] + CMP_LEN, ss[:, None] + SLC_LEN) - np.maximum(sc[None, :], ss[:, None])
    ov = np.clip(ov, 0, None) / CMP_LEN
    ov[:, ncp - 1] = 0.0
    return jnp.asarray(ov, dtype=MXU_DT)


def _layer(x2, B, T, layer, n_mix_pre, n_mix_post, n_ffn_pre, n_ffn_post, w_in, f_bias,
           ck_pe, ck_w1, ck_w2, cv_pe, cv_w1, cv_w2, stacked, tables):
    ct, st, cc, sc, ovt = tables
    w_up_fox, w_up_nsa, w_out, w_gate, w_up, w_down = stacked
    row = lambda v: v.reshape(1, -1).astype(F32)
    bf = lambda w: w.astype(MXU_DT)
    proj = _inproj(x2, row(n_mix_pre), _pack_w_in(w_in))
    bias_row = jnp.pad(f_bias.astype(F32), (0, LANES - FOX_HEADS)).reshape(1, LANES)
    cum = _foxcum(proj, bias_row, B, T)
    o_fox = _fox(proj, cum, B, T)
    kc, vc = _compress(proj, ck_pe, bf(ck_w1), bf(ck_w2), cv_pe, bf(cv_w1), bf(cv_w2), cc, sc, B, T)
    o_nsa = _nsa(proj, kc, vc, ct, st, ovt, B, T)
    mix = _merge(o_fox, o_nsa, w_up_fox, w_up_nsa, layer, proj)
    x2 = _outproj(mix, w_out, layer, row(n_mix_post), x2)
    h = _ffn_up(x2, row(n_ffn_pre), w_gate, w_up, layer)
    return _ffn_down(h, w_down, layer, row(n_ffn_post), x2)


@jax.jit
def kernel(x, norm_mix_pre, norm_mix_post, norm_ffn_pre, norm_ffn_post, w_in, fox_forget_bias, cmp_k_pe, cmp_k_w1, cmp_k_w2, cmp_v_pe, cmp_v_w1, cmp_v_w2, w_up_fox, w_up_nsa, w_out, w_ffn_gate, w_ffn_up, w_ffn_down):
    B, T, D = x.shape
    ct, st = _rope_tables(jnp.arange(T, dtype=F32))
    ncp = T // CMP_STRIDE
    cc, sc = _rope_tables((jnp.arange(ncp) * CMP_STRIDE + CMP_LEN - 1).astype(F32))
    tables = (ct, st, cc, sc, _overlap_t(T))
    x2 = x.reshape(B * T, D)
    bf = lambda w: w.astype(MXU_DT)
    stacked = (bf(w_up_fox), bf(w_up_nsa), bf(w_out), w_ffn_gate, w_ffn_up, bf(w_ffn_down))
    for l in range(w_in.shape[0]):
        x2 = _layer(x2, B, T, l, norm_mix_pre[l], norm_mix_post[l], norm_ffn_pre[l], norm_ffn_post[l],
                    w_in[l], fox_forget_bias[l],
                    cmp_k_pe[l], cmp_k_w1[l], cmp_k_w2[l], cmp_v_pe[l], cmp_v_w1[l], cmp_v_w2[l],
                    stacked, tables)
    return x2.reshape(B, T, D)
```

```python
import functools

import numpy as np
import jax
import jax.numpy as jnp
from jax import lax
from jax.experimental import pallas as pl
from jax.experimental.pallas import tpu as pltpu

D_MODEL = 2048
HEAD_DIM = 128
FOX_HEADS = 8
NSA_HEADS = 8
NSA_GROUP_SIZE = 4
NSA_KV_GROUPS = 2
FOX_W = FOX_HEADS * HEAD_DIM
NSA_W = NSA_HEADS * HEAD_DIM
NSA_KV_W = NSA_KV_GROUPS * HEAD_DIM
ROPE_DIM = HEAD_DIM // 4
ROPE_HALF = ROPE_DIM // 2
ROPE_THETA = 500000.0
CMP_LEN = 32
CMP_STRIDE = 16
CMP_HIDDEN = 256
SLC_LEN = 64
SLC_SHIFT = SLC_LEN.bit_length() - 1
CMP_SHIFT = CMP_STRIDE.bit_length() - 1
SLC_TOP = 16
WINDOW = 512
FFN_HIDDEN = 5632
EPS = 1e-6
NEG_INF = -1e30
FORCE_SCORE = 1e6
SCALE = HEAD_DIM ** -0.5
LOG2E = float(np.log2(np.e))
QSCALE = SCALE * LOG2E
IN_SIZES = [FOX_W, FOX_W, FOX_W, FOX_HEADS, NSA_W,
            NSA_KV_W, NSA_KV_W, NSA_KV_W, NSA_KV_W, NSA_KV_W, NSA_KV_W,
            NSA_HEADS * 3, D_MODEL, D_MODEL]

LANES = 128
MXU_DT = jnp.bfloat16
VMEM_LIMIT = 56 * 1024 * 1024

CB_GF, CB_GN = 0, 16
CB_FQ, CB_FK, CB_FV, CB_NQ = 32, 40, 48, 56
CB_NKC, CB_NVC, CB_NKS, CB_NVS, CB_NKW, CB_NVW = 64, 66, 68, 70, 72, 74
CB_SM = 76
NP_COLS = 78 * LANES
SM_FF, SM_GATE = 0, FOX_HEADS
MXU_N = 256

F32 = jnp.float32


def _cparams(n_axes):
    return pltpu.CompilerParams(dimension_semantics=("arbitrary",) * n_axes,
                                vmem_limit_bytes=VMEM_LIMIT)


def _rms(x):
    return x * lax.rsqrt(jnp.mean(x * x, axis=-1, keepdims=True) + EPS)


def _dot(a, b):
    return jnp.dot(a, b, preferred_element_type=F32)


def _dot_nt(a, b):
    return lax.dot_general(a, b, (((1,), (1,)), ((), ())), preferred_element_type=F32)


def _split3(x):
    hi = x.astype(MXU_DT)
    r1 = x - hi.astype(F32)
    mid = r1.astype(MXU_DT)
    lo = (r1 - mid.astype(F32)).astype(MXU_DT)
    return hi, mid, lo


def _rope(x, c, s):
    lane = lax.broadcasted_iota(jnp.int32, x.shape, 1)
    partner = jnp.where(lane < ROPE_HALF,
                        pltpu.roll(x, LANES - ROPE_HALF, 1),
                        pltpu.roll(x, ROPE_HALF, 1))
    return x * c + partner * s


def _sigmoid(z):
    return 1.0 / (1.0 + jnp.exp(-z))


def _inproj_body(x_ref, g_ref, w_ref, o_ref, a_ref):
    @pl.when(pl.program_id(1) == 0)
    def _():
        a_ref[...] = (_rms(x_ref[...]) * g_ref[...]).astype(a_ref.dtype)

    o_ref[...] = _dot(a_ref[...], w_ref[...])


def _inproj(x2, g, w):
    m, d = x2.shape
    n = w.shape[1]
    tm = min(1024, m)
    tn = 13 * LANES
    return pl.pallas_call(
        _inproj_body,
        out_shape=jax.ShapeDtypeStruct((m, n), F32),
        grid=(m // tm, n // tn),
        in_specs=[pl.BlockSpec((tm, d), lambda i, j: (i, 0)),
                  pl.BlockSpec((1, d), lambda i, j: (0, 0)),
                  pl.BlockSpec((d, tn), lambda i, j: (0, j))],
        out_specs=pl.BlockSpec((tm, tn), lambda i, j: (i, j)),
        scratch_shapes=[pltpu.VMEM((tm, d), MXU_DT)],
        compiler_params=_cparams(2),
        name="inproj",
    )(x2, g, w)


def _foxcum_body(s_ref, b_ref, cum_ref, *, T, CH):
    r = lax.broadcasted_iota(jnp.int32, (CH, CH), 0)
    c = lax.broadcasted_iota(jnp.int32, (CH, CH), 1)
    tri = jnp.where(c <= r, 1.0, 0.0).astype(MXU_DT)
    carry = jnp.zeros((1, LANES), F32)
    for ci in range(T // CH):
        z = s_ref[ci * CH:(ci + 1) * CH, :] + b_ref[...]
        lf = jnp.minimum(z, 0.0) - jnp.log1p(jnp.exp(-jnp.abs(z)))
        hi, mid, lo = _split3(lf)
        out = (_dot(tri, hi) + _dot(tri, mid)) + _dot(tri, lo) + carry
        cum_ref[ci * CH:(ci + 1) * CH, :] = out
        carry = out[CH - 1:CH, :]


def _foxcum(proj, bias_row, B, T):
    CH = 256
    return pl.pallas_call(
        functools.partial(_foxcum_body, T=T, CH=CH),
        out_shape=jax.ShapeDtypeStruct((B * T, LANES), F32),
        grid=(B,),
        in_specs=[pl.BlockSpec((T, LANES), lambda b: (b, CB_SM)),
                  pl.BlockSpec((1, LANES), lambda b: (0, 0))],
        out_specs=pl.BlockSpec((T, LANES), lambda b: (b, 0)),
        compiler_params=_cparams(1),
        name="foxcum",
    )(proj, bias_row)


def _online_softmax_step_t(st, vt_t, m_ref, l_ref, acc_ref):
    m_old = m_ref[...]
    m_new = jnp.maximum(m_old, jnp.max(st, axis=0, keepdims=True))
    alpha = jnp.exp2(m_old - m_new)
    p = jnp.exp2(st - m_new)
    l_ref[...] = alpha * l_ref[...] + jnp.sum(p, axis=0, keepdims=True)
    acc_ref[...] = alpha * acc_ref[...] + _dot(vt_t, p.astype(MXU_DT))
    m_ref[...] = m_new


def _fox_body(q_ref, k_ref, v_ref, cum_ref, o_ref,
              ka_ref, vt_ref, m_ref, l_ref, acc_ref, sa_ref, sb_ref, *, T, TQ, NH):
    hp = pl.program_id(1)
    i = pl.program_id(2)
    RC = 512
    heads = range(NH)

    def cum_pieces(hh, rows, n):
        lane = lax.broadcasted_iota(jnp.int32, (n, LANES), 1)
        col = jnp.sum(jnp.where(lane == hp * NH + hh, cum_ref[rows, :], 0.0), axis=1, keepdims=True)
        return [p.astype(F32) for p in _split3(col * LOG2E)]

    @pl.when(i == 0)
    def _():
        for hh in heads:
            hcols = slice(hh * HEAD_DIM, (hh + 1) * HEAD_DIM)
            for c0 in range(0, T, RC):
                sl = slice(c0, c0 + RC)
                ka_ref[hh, sl, 0:HEAD_DIM] = k_ref[sl, hcols].astype(MXU_DT)
                lane = lax.broadcasted_iota(jnp.int32, (RC, LANES), 1)
                aug = jnp.where((lane >= 3) & (lane < 6), 1.0, 0.0)
                for idx, c in enumerate(cum_pieces(hh, sl, RC)):
                    aug = jnp.where(lane == idx, c, aug)
                ka_ref[hh, sl, HEAD_DIM:HEAD_DIM + LANES] = aug.astype(MXU_DT)
                vt_ref[hh, :, sl] = v_ref[sl, hcols].T.astype(MXU_DT)

    q0 = pl.multiple_of(i * TQ, TQ)
    lane = lax.broadcasted_iota(jnp.int32, (TQ, LANES), 1)
    qa = []
    for hh in heads:
        qaug = jnp.where(lane < 3, -1.0, 0.0)
        for idx, c in enumerate(cum_pieces(hh, pl.ds(q0, TQ), TQ)):
            qaug = jnp.where(lane == 3 + idx, c, qaug)
        qh = q_ref[:, hh * HEAD_DIM:(hh + 1) * HEAD_DIM] * QSCALE
        qa.append(jnp.concatenate([qh.astype(MXU_DT), qaug.astype(MXU_DT)], axis=1))
    m_ref[...] = jnp.full(m_ref.shape, NEG_INF, F32)
    l_ref[...] = jnp.zeros(l_ref.shape, F32)
    acc_ref[...] = jnp.zeros(acc_ref.shape, F32)

    def scores(t, s_ref):
        k0 = pl.multiple_of(t * TQ, TQ)
        for hh in heads:
            s_ref[hh] = _dot_nt(ka_ref[hh, pl.ds(k0, TQ), :], qa[hh])

    def consume(t, s_ref, diagonal):
        k0 = pl.multiple_of(t * TQ, TQ)
        for hh in heads:
            st = s_ref[hh]
            if diagonal:
                krow = lax.broadcasted_iota(jnp.int32, (TQ, TQ), 0)
                qcol = lax.broadcasted_iota(jnp.int32, (TQ, TQ), 1)
                st = jnp.where(krow <= qcol, st, NEG_INF)
            _online_softmax_step_t(st, vt_ref[hh, :, pl.ds(k0, TQ)],
                                   m_ref.at[hh], l_ref.at[hh], acc_ref.at[hh])

    scores(0, sa_ref)

    def pair(jj, carry):
        scores(2 * jj + 1, sb_ref)
        consume(2 * jj, sa_ref, False)
        scores(2 * jj + 2, sa_ref)
        consume(2 * jj + 1, sb_ref, False)
        return carry

    lax.fori_loop(0, i // 2, pair, 0)

    @pl.when(i % 2 == 0)
    def _():
        consume(i, sa_ref, True)

    @pl.when(i % 2 == 1)
    def _():
        scores(i, sb_ref)
        consume(i - 1, sa_ref, False)
        consume(i, sb_ref, True)

    for hh in heads:
        o_ref[:, hh * HEAD_DIM:(hh + 1) * HEAD_DIM] = (acc_ref[hh] / l_ref[hh]).T.astype(o_ref.dtype)


def _fox(proj, cum, B, T):
    TQ = 512
    NH = 4
    nq = T // TQ
    hw = NH * HEAD_DIM
    one = pl.Buffered(1)
    return pl.pallas_call(
        functools.partial(_fox_body, T=T, TQ=TQ, NH=NH),
        out_shape=jax.ShapeDtypeStruct((B * T, FOX_W), MXU_DT),
        grid=(B, FOX_HEADS // NH, nq),
        in_specs=[pl.BlockSpec((TQ, hw), lambda b, h, i: (b * nq + i, CB_FQ // NH + h)),
                  pl.BlockSpec((T, hw), lambda b, h, i: (b, CB_FK // NH + h), pipeline_mode=one),
                  pl.BlockSpec((T, hw), lambda b, h, i: (b, CB_FV // NH + h), pipeline_mode=one),
                  pl.BlockSpec((T, LANES), lambda b, h, i: (b, 0), pipeline_mode=one)],
        out_specs=pl.BlockSpec((TQ, hw), lambda b, h, i: (b * nq + i, h)),
        scratch_shapes=[pltpu.VMEM((NH, T, HEAD_DIM + LANES), MXU_DT),
                        pltpu.VMEM((NH, HEAD_DIM, T), MXU_DT),
                        pltpu.VMEM((NH, 1, TQ), F32),
                        pltpu.VMEM((NH, 1, TQ), F32),
                        pltpu.VMEM((NH, HEAD_DIM, TQ), F32),
                        pltpu.VMEM((NH, TQ, TQ), F32),
                        pltpu.VMEM((NH, TQ, TQ), F32)],
        compiler_params=_cparams(3),
        name="fox",
    )(proj, proj, proj, cum)


def _gelu_tanh(x):
    return 0.5 * x * (1.0 + jnp.tanh(np.sqrt(2.0 / np.pi).astype(np.float32)
                                     * (x + 0.044715 * (x * x * x))))


def _compress_one(x_ref, pe_ref, w1_ref, w2_ref, NCP):
    h0 = jnp.zeros((NCP, CMP_HIDDEN), F32)
    h1 = jnp.zeros((NCP, CMP_HIDDEN), F32)
    for l in range(CMP_STRIDE):
        xl = x_ref[pl.ds(l, NCP, stride=CMP_STRIDE), :]
        a0 = (xl + pe_ref[l:l + 1, :]).astype(MXU_DT)
        a1 = (xl + pe_ref[CMP_STRIDE + l:CMP_STRIDE + l + 1, :]).astype(MXU_DT)
        h0 = h0 + _dot(a0, w1_ref[l * HEAD_DIM:(l + 1) * HEAD_DIM, :])
        h1 = h1 + _dot(a1, w1_ref[(CMP_STRIDE + l) * HEAD_DIM:(CMP_STRIDE + l + 1) * HEAD_DIM, :])
    hsum = h0 + pltpu.roll(h1, NCP - 1, 0)
    return _dot(_gelu_tanh(hsum).astype(MXU_DT), w2_ref[...])


def _compress_body(xk_ref, xv_ref, pek_ref, w1k_ref, w2k_ref, pev_ref, w1v_ref, w2v_ref,
                   cc_ref, sc_ref, kc_ref, vc_ref, *, NCP):
    kc = _compress_one(xk_ref, pek_ref, w1k_ref, w2k_ref, NCP)
    kc_ref[0, 0] = _rope(kc, cc_ref[...], sc_ref[...]).astype(kc_ref.dtype)
    vc = _compress_one(xv_ref, pev_ref, w1v_ref, w2v_ref, NCP)
    vc_ref[0, 0] = vc.T.astype(vc_ref.dtype)


def _compress(proj, pek, w1k, w2k, pev, w1v, w2v, cc, sc, B, T):
    NCP = T // CMP_STRIDE
    full = lambda shape: pl.BlockSpec(shape, lambda b, g: (0,) * len(shape))
    k_sds = jax.ShapeDtypeStruct((B, NSA_KV_GROUPS, NCP, HEAD_DIM), MXU_DT)
    v_sds = jax.ShapeDtypeStruct((B, NSA_KV_GROUPS, HEAD_DIM, NCP), MXU_DT)
    k_spec = pl.BlockSpec((1, 1, NCP, HEAD_DIM), lambda b, g: (b, g, 0, 0))
    v_spec = pl.BlockSpec((1, 1, HEAD_DIM, NCP), lambda b, g: (b, g, 0, 0))
    return pl.pallas_call(
        functools.partial(_compress_body, NCP=NCP),
        out_shape=(k_sds, v_sds),
        grid=(B, NSA_KV_GROUPS),
        in_specs=[pl.BlockSpec((T, LANES), lambda b, g: (b, CB_NKC + g)),
                  pl.BlockSpec((T, LANES), lambda b, g: (b, CB_NVC + g)),
                  full((CMP_LEN, HEAD_DIM)), full((CMP_LEN * HEAD_DIM, CMP_HIDDEN)),
                  full((CMP_HIDDEN, HEAD_DIM)),
                  full((CMP_LEN, HEAD_DIM)), full((CMP_LEN * HEAD_DIM, CMP_HIDDEN)),
                  full((CMP_HIDDEN, HEAD_DIM)),
                  full((NCP, LANES)), full((NCP, LANES))],
        out_specs=(k_spec, v_spec),
        compiler_params=_cparams(2),
        name="compress",
    )(proj, proj, pek, w1k, w2k, pev, w1v, w2v, cc, sc)


def _nsa_body(q_ref, sm_ref, cq_ref, sq_ref, kc_ref, vc_ref, ks_ref, vs_ref, kw_ref, vw_ref,
              ct_ref, st_ref, ovt_ref, o_ref,
              ksa_ref, vst_ref, kwb_ref, vwt_ref, smt_ref, m_ref, l_ref, acc_ref, sa_ref, sb_ref,
              part_ref, *, T, TQ, TK, WSP, NB, NCP, NSEL):
    i = pl.program_id(1)
    HG = NSA_GROUP_SIZE
    NG = NSA_KV_GROUPS
    groups = range(NG)
    R = HG * TQ
    RC = 512

    @pl.when(i == 0)
    def _():
        for c0 in range(0, T, RC):
            sl = slice(c0, c0 + RC)
            ct = ct_ref[sl, :]
            st = st_ref[sl, :]
            kblk = (c0 + lax.broadcasted_iota(jnp.int32, (RC, LANES), 0)) >> SLC_SHIFT
            onehot = jnp.where(kblk == lax.broadcasted_iota(jnp.int32, (RC, LANES), 1), 1.0, 0.0)
            for gg in groups:
                gc = slice(gg * HEAD_DIM, (gg + 1) * HEAD_DIM)
                ksa_ref[gg, sl, 0:HEAD_DIM] = _rope(ks_ref[sl, gc], ct, st).astype(MXU_DT)
                ksa_ref[gg, sl, HEAD_DIM:HEAD_DIM + LANES] = onehot.astype(MXU_DT)
                kwb_ref[gg, sl, :] = _rope(kw_ref[sl, gc], ct, st).astype(MXU_DT)
                vst_ref[gg, :, sl] = vs_ref[sl, gc].T.astype(MXU_DT)
                vwt_ref[gg, :, sl] = vw_ref[sl, gc].T.astype(MXU_DT)

    q0 = i * TQ
    cq = cq_ref[...]
    sq = sq_ref[...]
    qpos = q0 + (lax.broadcasted_iota(jnp.int32, (1, R), 1) & (TQ - 1))
    smt_ref[...] = sm_ref[...].T

    def gate_row(gg, c):
        return jnp.concatenate(
            [_sigmoid(smt_ref[SM_GATE + 3 * (gg * HG + hh) + c:SM_GATE + 3 * (gg * HG + hh) + c + 1, :])
             for hh in range(HG)], axis=1)

    w0 = pl.multiple_of(jnp.clip((i + 1) * TQ - WSP, 0, T - WSP), LANES)
    dist = (q0 + lax.broadcasted_iota(jnp.int32, (WSP, TQ), 1)) \
        - (w0 + lax.broadcasted_iota(jnp.int32, (WSP, TQ), 0))
    wbias = jnp.where((dist >= 0) & (dist < WINDOW), 0.0, NEG_INF)
    wbias = jnp.concatenate([wbias] * HG, axis=1)
    ovt = ovt_ref[...]
    jblk = lax.broadcasted_iota(jnp.int32, (NB, TQ), 0)
    cur = (q0 + lax.broadcasted_iota(jnp.int32, (NB, TQ), 1)) >> SLC_SHIFT
    forced = (jblk == 0) | (jblk == cur) | (jblk == cur - 1)
    future = jblk > cur
    SUB = 8
    sub = lax.broadcasted_iota(jnp.int32, (SUB, TQ), 0)

    qa = []
    for gg in groups:
        qb = jnp.concatenate(
            [_rope(q_ref[:, (gg * HG + hh) * HEAD_DIM:(gg * HG + hh + 1) * HEAD_DIM], cq, sq) * QSCALE
             for hh in range(HG)], axis=0).astype(MXU_DT)

        sc = _dot_nt(kc_ref[0, gg], qb)
        nrow = lax.broadcasted_iota(jnp.int32, (NCP, R), 0)
        last_valid = (qpos - (CMP_LEN - 1)) >> CMP_SHIFT
        sc = jnp.where(nrow <= last_valid, sc, NEG_INF)
        pc = jnp.exp2(sc - jnp.max(sc, axis=0, keepdims=True))
        pc = pc * (jnp.where(qpos >= CMP_LEN - 1, 1.0, 0.0) / jnp.sum(pc, axis=0, keepdims=True))
        o_c = _dot(vc_ref[0, gg], pc.astype(MXU_DT))

        sw = _dot_nt(kwb_ref[gg, pl.ds(w0, WSP), :], qb) + wbias
        pw = jnp.exp2(sw - jnp.max(sw, axis=0, keepdims=True))
        o_w = _dot(vwt_ref[gg, :, pl.ds(w0, WSP)], pw.astype(MXU_DT)) / jnp.sum(pw, axis=0, keepdims=True)
        part_ref[gg] = gate_row(gg, 0) * o_c + gate_row(gg, 2) * o_w

        psum = (pc[:, 0:TQ] + pc[:, TQ:2 * TQ]) + (pc[:, 2 * TQ:3 * TQ] + pc[:, 3 * TQ:4 * TQ])
        hi, mid, lo = _split3(psum)
        imp = (_dot(ovt, hi) + _dot(ovt, mid)) + _dot(ovt, lo)
        imp = jnp.where(forced, FORCE_SCORE, jnp.where(future, -1.0, imp))
        blocks = [imp[r * SUB:(r + 1) * SUB, :] for r in range(NB // SUB)]
        ranks = [jnp.zeros((SUB, TQ), F32) for _ in blocks]
        for jp in range(NB):
            gj, sj = divmod(jp, SUB)
            row = blocks[gj][sj:sj + 1, :]
            for r, blk in enumerate(blocks):
                if r > gj:
                    beats = jnp.where(row >= blk, 1.0, 0.0)
                elif r < gj:
                    beats = jnp.where(row > blk, 1.0, 0.0)
                else:
                    beats = jnp.where(sub > sj, jnp.where(row >= blk, 1.0, 0.0),
                                      jnp.where(row > blk, 1.0, 0.0))
                ranks[r] = ranks[r] + beats
        rank = jnp.concatenate(ranks, axis=0)
        sel_t = jnp.where((rank < NSEL) & (imp >= 0.0), 1.0, 0.0)
        if NB < LANES:
            sel_t = jnp.concatenate([sel_t, jnp.zeros((LANES - NB, TQ), F32)], axis=0)
        selneg = jnp.where(sel_t.T > 0.5, 0.0, NEG_INF).astype(MXU_DT)
        qa.append(jnp.concatenate([qb, jnp.concatenate([selneg] * HG, axis=0)], axis=1))

    m_ref[...] = jnp.full(m_ref.shape, NEG_INF, F32)
    l_ref[...] = jnp.zeros(l_ref.shape, F32)
    acc_ref[...] = jnp.zeros(acc_ref.shape, F32)

    def scores(t, s_ref):
        k0 = pl.multiple_of(t * TK, TK)
        for gg in groups:
            s_ref[gg] = _dot_nt(ksa_ref[gg, pl.ds(k0, TK), :], qa[gg])

    def consume(t, s_ref, causal):
        k0 = pl.multiple_of(t * TK, TK)
        for gg in groups:
            if causal:
                own = pl.ds(pl.multiple_of(q0 - k0, TQ), TQ)
                krow = lax.broadcasted_iota(jnp.int32, (TQ, TQ), 0)
                qcol = lax.broadcasted_iota(jnp.int32, (TQ, TQ), 1)
                tri = jnp.where(krow <= qcol, 0.0, NEG_INF)
                s_ref[gg, own, :] = s_ref[gg, own, :] + jnp.concatenate([tri] * HG, axis=1)
            _online_softmax_step_t(s_ref[gg], vst_ref[gg, :, pl.ds(k0, TK)],
                                   m_ref.at[gg], l_ref.at[gg], acc_ref.at[gg])

    last = ((i + 1) * TQ + TK - 1) // TK - 1
    scores(0, sa_ref)

    def pair(jj, carry):
        scores(2 * jj + 1, sb_ref)
        consume(2 * jj, sa_ref, False)
        scores(2 * jj + 2, sa_ref)
        consume(2 * jj + 1, sb_ref, False)
        return carry

    lax.fori_loop(0, last // 2, pair, 0)

    @pl.when(last % 2 == 0)
    def _():
        consume(last, sa_ref, True)

    @pl.when(last % 2 == 1)
    def _():
        scores(last, sb_ref)
        consume(last - 1, sa_ref, False)
        consume(last, sb_ref, True)

    for gg in groups:
        o = part_ref[gg] + (gate_row(gg, 1) / l_ref[gg]) * acc_ref[gg]
        for hh in range(HG):
            oc = (gg * HG + hh) * HEAD_DIM
            o_ref[:, oc:oc + HEAD_DIM] = o[:, hh * TQ:(hh + 1) * TQ].T.astype(o_ref.dtype)


def _nsa(proj, kc, vc, ct, st, ovt, B, T):
    TQ = 128
    TK = 512
    WSP = WINDOW + TQ
    NB = T // SLC_LEN
    NCP = T // CMP_STRIDE
    nq = T // TQ
    HG = NSA_GROUP_SIZE
    NG = NSA_KV_GROUPS
    R = HG * TQ
    one = pl.Buffered(1)
    kv_spec = lambda cb: pl.BlockSpec((T, NG * LANES), lambda b, i: (b, cb // NG), pipeline_mode=one)
    return pl.pallas_call(
        functools.partial(_nsa_body, T=T, TQ=TQ, TK=TK, WSP=WSP, NB=NB, NCP=NCP,
                          NSEL=min(SLC_TOP, NB)),
        out_shape=jax.ShapeDtypeStruct((B * T, NSA_W), MXU_DT),
        grid=(B, nq),
        in_specs=[pl.BlockSpec((TQ, NSA_W), lambda b, i: (b * nq + i, CB_NQ * LANES // NSA_W)),
                  pl.BlockSpec((TQ, LANES), lambda b, i: (b * nq + i, CB_SM)),
                  pl.BlockSpec((TQ, LANES), lambda b, i: (i, 0)),
                  pl.BlockSpec((TQ, LANES), lambda b, i: (i, 0)),
                  pl.BlockSpec((1, NG, NCP, HEAD_DIM), lambda b, i: (b, 0, 0, 0)),
                  pl.BlockSpec((1, NG, HEAD_DIM, NCP), lambda b, i: (b, 0, 0, 0)),
                  kv_spec(CB_NKS), kv_spec(CB_NVS), kv_spec(CB_NKW), kv_spec(CB_NVW),
                  pl.BlockSpec((T, LANES), lambda b, i: (0, 0), pipeline_mode=one),
                  pl.BlockSpec((T, LANES), lambda b, i: (0, 0), pipeline_mode=one),
                  pl.BlockSpec((NB, NCP), lambda b, i: (0, 0))],
        out_specs=pl.BlockSpec((TQ, NSA_W), lambda b, i: (b * nq + i, 0)),
        scratch_shapes=[pltpu.VMEM((NG, T, HEAD_DIM + LANES), MXU_DT),
                        pltpu.VMEM((NG, HEAD_DIM, T), MXU_DT),
                        pltpu.VMEM((NG, T, LANES), MXU_DT),
                        pltpu.VMEM((NG, HEAD_DIM, T), MXU_DT),
                        pltpu.VMEM((LANES, TQ), F32),
                        pltpu.VMEM((NG, 1, R), F32),
                        pltpu.VMEM((NG, 1, R), F32),
                        pltpu.VMEM((NG, HEAD_DIM, R), F32),
                        pltpu.VMEM((NG, TK, R), F32),
                        pltpu.VMEM((NG, TK, R), F32),
                        pltpu.VMEM((NG, HEAD_DIM, R), F32)],
        compiler_params=_cparams(2),
        name="nsa",
    )(proj, proj, ct, st, kc, vc, proj, proj, proj, proj, ct, st, ovt)


def _merge_body(of_ref, on_ref, wf_ref, wn_ref, gf_ref, gn_ref, o_ref, *, CN):
    of = of_ref[...]
    on = on_ref[...]
    for c0 in range(0, D_MODEL, CN):
        cols = slice(c0, c0 + CN)
        yf = _dot(of, wf_ref[:, cols])
        yn = _dot(on, wn_ref[:, cols])
        o_ref[:, cols] = (_sigmoid(gf_ref[:, cols]) * yf + _sigmoid(gn_ref[:, cols]) * yn).astype(o_ref.dtype)


def _merge(o_fox, o_nsa, wf, wn, layer, proj):
    m = o_fox.shape[0]
    tm = min(512, m)
    d = D_MODEL
    one = pl.Buffered(1)
    return pl.pallas_call(
        functools.partial(_merge_body, CN=2 * MXU_N),
        out_shape=jax.ShapeDtypeStruct((m, d), MXU_DT),
        grid=(m // tm,),
        in_specs=[pl.BlockSpec((tm, FOX_W), lambda i: (i, 0)),
                  pl.BlockSpec((tm, NSA_W), lambda i: (i, 0)),
                  pl.BlockSpec((None, FOX_W, d), lambda i: (layer, 0, 0), pipeline_mode=one),
                  pl.BlockSpec((None, NSA_W, d), lambda i: (layer, 0, 0), pipeline_mode=one),
                  pl.BlockSpec((tm, d), lambda i: (i, CB_GF * LANES // d)),
                  pl.BlockSpec((tm, d), lambda i: (i, CB_GN * LANES // d))],
        out_specs=pl.BlockSpec((tm, d), lambda i: (i, 0)),
        compiler_params=_cparams(1),
        name="merge",
    )(o_fox, o_nsa, wf, wn, proj, proj)


def _outproj_body(a_ref, w_ref, g_ref, x_ref, o_ref):
    y = _dot(a_ref[...], w_ref[...])
    o_ref[...] = x_ref[...] + _rms(y) * g_ref[...]


def _outproj(mix, w, layer, g, x2):
    m, d = x2.shape
    tm = min(512, m)
    return pl.pallas_call(
        _outproj_body,
        out_shape=jax.ShapeDtypeStruct((m, d), F32),
        grid=(m // tm,),
        in_specs=[pl.BlockSpec((tm, d), lambda i: (i, 0)),
                  pl.BlockSpec((None, d, d), lambda i: (layer, 0, 0)),
                  pl.BlockSpec((1, d), lambda i: (0, 0)),
                  pl.BlockSpec((tm, d), lambda i: (i, 0))],
        out_specs=pl.BlockSpec((tm, d), lambda i: (i, 0)),
        compiler_params=_cparams(1),
        name="outproj",
    )(mix, w, g, x2)


def _ffn_up_body(x_ref, g_ref, wg_ref, wu_ref, o_ref, a_ref):
    @pl.when(pl.program_id(1) == 0)
    def _():
        a_ref[...] = (_rms(x_ref[...]) * g_ref[...]).astype(a_ref.dtype)

    a = a_ref[...]
    hg = _dot(a, wg_ref[...].astype(MXU_DT))
    hu = _dot(a, wu_ref[...].astype(MXU_DT))
    o_ref[...] = (hg * _sigmoid(hg) * hu).astype(o_ref.dtype)


def _ffn_up(x2, g, wg, wu, layer):
    m, d = x2.shape
    f = wg.shape[2]
    tm = min(1024, m)
    tn = 512
    return pl.pallas_call(
        _ffn_up_body,
        out_shape=jax.ShapeDtypeStruct((m, f), MXU_DT),
        grid=(m // tm, f // tn),
        in_specs=[pl.BlockSpec((tm, d), lambda i, j: (i, 0)),
                  pl.BlockSpec((1, d), lambda i, j: (0, 0)),
                  pl.BlockSpec((None, d, tn), lambda i, j: (layer, 0, j)),
                  pl.BlockSpec((None, d, tn), lambda i, j: (layer, 0, j))],
        out_specs=pl.BlockSpec((tm, tn), lambda i, j: (i, j)),
        scratch_shapes=[pltpu.VMEM((tm, d), MXU_DT)],
        compiler_params=_cparams(2),
        name="ffn_up",
    )(x2, g, wg, wu)


def _ffn_down_body(h_ref, w_ref, g_ref, x_ref, o_ref):
    y = _dot(h_ref[...], w_ref[...])
    o_ref[...] = x_ref[...] + _rms(y) * g_ref[...]


def _ffn_down(h, w, layer, g, x2):
    m, d = x2.shape
    f = h.shape[1]
    tm = min(256, m)
    return pl.pallas_call(
        _ffn_down_body,
        out_shape=jax.ShapeDtypeStruct((m, d), F32),
        grid=(m // tm,),
        in_specs=[pl.BlockSpec((tm, f), lambda i: (i, 0)),
                  pl.BlockSpec((None, f, d), lambda i: (layer, 0, 0), pipeline_mode=pl.Buffered(1)),
                  pl.BlockSpec((1, d), lambda i: (0, 0)),
                  pl.BlockSpec((tm, d), lambda i: (i, 0))],
        out_specs=pl.BlockSpec((tm, d), lambda i: (i, 0)),
        compiler_params=_cparams(1),
        name="ffn_down",
    )(h, w, g, x2)


def _pack_body(w_ref, o_ref):
    offs = [int(v) for v in np.cumsum([0] + IN_SIZES)]
    n_in = offs[-1]

    def cols(a, b):
        a0 = (a // LANES) * LANES
        b1 = min(-(-b // LANES) * LANES, n_in)
        return w_ref[:, a0:b1][:, a - a0:b - a0]

    rows = o_ref.shape[0]
    gates = cols(offs[12], offs[14])
    fox_qkv = cols(offs[0], offs[3])
    nsa_qkv = cols(offs[4], offs[11])
    small = jnp.concatenate(
        [cols(offs[3], offs[4]), cols(offs[11], offs[12]),
         jnp.zeros((rows, NP_COLS - CB_SM * LANES - FOX_HEADS - 3 * NSA_HEADS), F32)], axis=1)
    o_ref[:, CB_GF * LANES:CB_FQ * LANES] = gates.astype(o_ref.dtype)
    o_ref[:, CB_FQ * LANES:CB_NQ * LANES] = fox_qkv.astype(o_ref.dtype)
    o_ref[:, CB_NQ * LANES:CB_SM * LANES] = nsa_qkv.astype(o_ref.dtype)
    o_ref[:, CB_SM * LANES:NP_COLS] = small.astype(o_ref.dtype)


def _pack_w_in(w_all, layer):
    _, d, n_in = w_all.shape
    tr = 256
    return pl.pallas_call(
        _pack_body,
        out_shape=jax.ShapeDtypeStruct((d, NP_COLS), MXU_DT),
        grid=(d // tr,),
        in_specs=[pl.BlockSpec((None, tr, n_in), lambda r: (layer, r, 0))],
        out_specs=pl.BlockSpec((tr, NP_COLS), lambda r: (r, 0)),
        compiler_params=_cparams(1),
        name="pack_w_in",
    )(w_all)


def _rope_tables(pos):
    inv_freq = jnp.power(ROPE_THETA, -jnp.arange(ROPE_HALF, dtype=F32) * (2.0 / ROPE_DIM))
    ang = pos[:, None] * inv_freq[None, :]
    cos, sin = jnp.cos(ang), jnp.sin(ang)
    n = pos.shape[0]
    pad = HEAD_DIM - ROPE_DIM
    c = jnp.concatenate([cos, cos, jnp.ones((n, pad), F32)], axis=1)
    s = jnp.concatenate([-sin, sin, jnp.zeros((n, pad), F32)], axis=1)
    return c, s


def _overlap_t(T):
    nb, ncp = T // SLC_LEN, T // CMP_STRIDE
    sc = np.arange(ncp) * CMP_STRIDE
    ss = np.arange(nb) * SLC_LEN
    ov = np.minimum(sc[None, :] + CMP_LEN, ss[:, None] + SLC_LEN) - np.maximum(sc[None, :], ss[:, None])
    ov = np.clip(ov, 0, None) / CMP_LEN
    ov[:, ncp - 1] = 0.0
    return jnp.asarray(ov, dtype=MXU_DT)


def _layer(x2, B, T, layer, n_mix_pre, n_mix_post, n_ffn_pre, n_ffn_post, w_in, f_bias,
           ck_pe, ck_w1, ck_w2, cv_pe, cv_w1, cv_w2, stacked, tables):
    ct, st, cc, sc, ovt = tables
    w_up_fox, w_up_nsa, w_out, w_gate, w_up, w_down = stacked
    row = lambda v: v.reshape(1, -1).astype(F32)
    bf = lambda w: w.astype(MXU_DT)
    proj = _inproj(x2, row(n_mix_pre), _pack_w_in(w_in, layer))
    bias_row = jnp.pad(f_bias.astype(F32), (0, LANES - FOX_HEADS)).reshape(1, LANES)
    cum = _foxcum(proj, bias_row, B, T)
    o_fox = _fox(proj, cum, B, T)
    kc, vc = _compress(proj, ck_pe, bf(ck_w1), bf(ck_w2), cv_pe, bf(cv_w1), bf(cv_w2), cc, sc, B, T)
    o_nsa = _nsa(proj, kc, vc, ct, st, ovt, B, T)
    mix = _merge(o_fox, o_nsa, w_up_fox, w_up_nsa, layer, proj)
    x2 = _outproj(mix, w_out, layer, row(n_mix_post), x2)
    h = _ffn_up(x2, row(n_ffn_pre), w_gate, w_up, layer)
    return _ffn_down(h, w_down, layer, row(n_ffn_post), x2)


@jax.jit
def kernel(x, norm_mix_pre, norm_mix_post, norm_ffn_pre, norm_ffn_post, w_in, fox_forget_bias, cmp_k_pe, cmp_k_w1, cmp_k_w2, cmp_v_pe, cmp_v_w1, cmp_v_w2, w_up_fox, w_up_nsa, w_out, w_ffn_gate, w_ffn_up, w_ffn_down):
    B, T, D = x.shape
    ct, st = _rope_tables(jnp.arange(T, dtype=F32))
    ncp = T // CMP_STRIDE
    cc, sc = _rope_tables((jnp.arange(ncp) * CMP_STRIDE + CMP_LEN - 1).astype(F32))
    tables = (ct, st, cc, sc, _overlap_t(T))
    x2 = x.reshape(B * T, D)
    bf = lambda w: w.astype(MXU_DT)
    stacked = (bf(w_up_fox), bf(w_up_nsa), bf(w_out), w_ffn_gate, w_ffn_up, bf(w_ffn_down))
    for l in range(w_in.shape[0]):
        x2 = _layer(x2, B, T, l, norm_mix_pre[l], norm_mix_post[l], norm_ffn_pre[l], norm_ffn_post[l],
                    w_in, fox_forget_bias[l],
                    cmp_k_pe[l], cmp_k_w1[l], cmp_k_w2[l], cmp_v_pe[l], cmp_v_w1[l], cmp_v_w2[l],
                    stacked, tables)
    return x2.reshape(B, T, D)
```

```python
import functools

import numpy as np
import jax
import jax.numpy as jnp
from jax import lax
from jax.experimental import pallas as pl
from jax.experimental.pallas import tpu as pltpu

D_MODEL = 2048
HEAD_DIM = 128
FOX_HEADS = 8
NSA_HEADS = 8
NSA_GROUP_SIZE = 4
NSA_KV_GROUPS = 2
FOX_W = FOX_HEADS * HEAD_DIM
NSA_W = NSA_HEADS * HEAD_DIM
NSA_KV_W = NSA_KV_GROUPS * HEAD_DIM
ROPE_DIM = HEAD_DIM // 4
ROPE_HALF = ROPE_DIM // 2
ROPE_THETA = 500000.0
CMP_LEN = 32
CMP_STRIDE = 16
CMP_HIDDEN = 256
SLC_LEN = 64
SLC_SHIFT = SLC_LEN.bit_length() - 1
CMP_SHIFT = CMP_STRIDE.bit_length() - 1
SLC_TOP = 16
WINDOW = 512
FFN_HIDDEN = 5632
EPS = 1e-6
NEG_INF = -1e30
FORCE_SCORE = 1e6
SCALE = HEAD_DIM ** -0.5
LOG2E = float(np.log2(np.e))
QSCALE = SCALE * LOG2E
IN_SIZES = [FOX_W, FOX_W, FOX_W, FOX_HEADS, NSA_W,
            NSA_KV_W, NSA_KV_W, NSA_KV_W, NSA_KV_W, NSA_KV_W, NSA_KV_W,
            NSA_HEADS * 3, D_MODEL, D_MODEL]

LANES = 128
MXU_DT = jnp.bfloat16
VMEM_LIMIT = 56 * 1024 * 1024

CB_GF, CB_GN = 0, 16
CB_FQ, CB_FK, CB_FV, CB_NQ = 32, 40, 48, 56
CB_NKC, CB_NVC, CB_NKS, CB_NVS, CB_NKW, CB_NVW = 64, 66, 68, 70, 72, 74
CB_SM = 76
NP_COLS = 78 * LANES
SM_FF, SM_GATE = 0, FOX_HEADS
MXU_N = 256

F32 = jnp.float32


def _cparams(n_axes):
    return pltpu.CompilerParams(dimension_semantics=("arbitrary",) * n_axes,
                                vmem_limit_bytes=VMEM_LIMIT)


def _rms(x):
    return x * lax.rsqrt(jnp.mean(x * x, axis=-1, keepdims=True) + EPS)


def _dot(a, b):
    return jnp.dot(a, b, preferred_element_type=F32)


def _dot_nt(a, b):
    return lax.dot_general(a, b, (((1,), (1,)), ((), ())), preferred_element_type=F32)


def _split3(x):
    hi = x.astype(MXU_DT)
    r1 = x - hi.astype(F32)
    mid = r1.astype(MXU_DT)
    lo = (r1 - mid.astype(F32)).astype(MXU_DT)
    return hi, mid, lo


def _rope(x, c, s):
    lane = lax.broadcasted_iota(jnp.int32, x.shape, 1)
    partner = jnp.where(lane < ROPE_HALF,
                        pltpu.roll(x, LANES - ROPE_HALF, 1),
                        pltpu.roll(x, ROPE_HALF, 1))
    return x * c + partner * s


def _sigmoid(z):
    return 1.0 / (1.0 + jnp.exp(-z))


def _inproj_body(x_ref, g_ref, w_ref, o_ref, a_ref):
    @pl.when(pl.program_id(1) == 0)
    def _():
        a_ref[...] = (_rms(x_ref[...]) * g_ref[...]).astype(a_ref.dtype)

    o_ref[...] = _dot(a_ref[...], w_ref[...])


def _inproj(x2, g, w):
    m, d = x2.shape
    n = w.shape[1]
    tm = min(1024, m)
    tn = 13 * LANES
    return pl.pallas_call(
        _inproj_body,
        out_shape=jax.ShapeDtypeStruct((m, n), F32),
        grid=(m // tm, n // tn),
        in_specs=[pl.BlockSpec((tm, d), lambda i, j: (i, 0)),
                  pl.BlockSpec((1, d), lambda i, j: (0, 0)),
                  pl.BlockSpec((d, tn), lambda i, j: (0, j))],
        out_specs=pl.BlockSpec((tm, tn), lambda i, j: (i, j)),
        scratch_shapes=[pltpu.VMEM((tm, d), MXU_DT)],
        compiler_params=_cparams(2),
        name="inproj",
    )(x2, g, w)


def _foxcum_body(s_ref, b_ref, cum_ref, *, T, CH):
    r = lax.broadcasted_iota(jnp.int32, (CH, CH), 0)
    c = lax.broadcasted_iota(jnp.int32, (CH, CH), 1)
    tri = jnp.where(c <= r, 1.0, 0.0).astype(MXU_DT)
    carry = jnp.zeros((1, LANES), F32)
    for ci in range(T // CH):
        z = s_ref[ci * CH:(ci + 1) * CH, :] + b_ref[...]
        lf = jnp.minimum(z, 0.0) - jnp.log1p(jnp.exp(-jnp.abs(z)))
        hi, mid, lo = _split3(lf)
        out = (_dot(tri, hi) + _dot(tri, mid)) + _dot(tri, lo) + carry
        cum_ref[ci * CH:(ci + 1) * CH, :] = out
        carry = out[CH - 1:CH, :]


def _foxcum(proj, bias_row, B, T):
    CH = 256
    return pl.pallas_call(
        functools.partial(_foxcum_body, T=T, CH=CH),
        out_shape=jax.ShapeDtypeStruct((B * T, LANES), F32),
        grid=(B,),
        in_specs=[pl.BlockSpec((T, LANES), lambda b: (b, CB_SM)),
                  pl.BlockSpec((1, LANES), lambda b: (0, 0))],
        out_specs=pl.BlockSpec((T, LANES), lambda b: (b, 0)),
        compiler_params=_cparams(1),
        name="foxcum",
    )(proj, bias_row)


def _online_softmax_step_t(st, vt_t, m_ref, l_ref, acc_ref):
    m_old = m_ref[...]
    m_new = jnp.maximum(m_old, jnp.max(st, axis=0, keepdims=True))
    alpha = jnp.exp2(m_old - m_new)
    p = jnp.exp2(st - m_new)
    l_ref[...] = alpha * l_ref[...] + jnp.sum(p, axis=0, keepdims=True)
    acc_ref[...] = alpha * acc_ref[...] + _dot(vt_t, p.astype(MXU_DT))
    m_ref[...] = m_new


def _fox_body(q_ref, k_ref, v_ref, cum_ref, o_ref,
              ka_ref, vt_ref, m_ref, l_ref, acc_ref, sa_ref, sb_ref, *, T, TQ, NH):
    hp = pl.program_id(1)
    i = pl.program_id(2)
    RC = 512
    heads = range(NH)

    def cum_pieces(hh, rows, n):
        lane = lax.broadcasted_iota(jnp.int32, (n, LANES), 1)
        col = jnp.sum(jnp.where(lane == hp * NH + hh, cum_ref[rows, :], 0.0), axis=1, keepdims=True)
        return [p.astype(F32) for p in _split3(col * LOG2E)]

    @pl.when(i == 0)
    def _():
        for hh in heads:
            hcols = slice(hh * HEAD_DIM, (hh + 1) * HEAD_DIM)
            for c0 in range(0, T, RC):
                sl = slice(c0, c0 + RC)
                ka_ref[hh, sl, 0:HEAD_DIM] = k_ref[sl, hcols].astype(MXU_DT)
                lane = lax.broadcasted_iota(jnp.int32, (RC, LANES), 1)
                aug = jnp.where((lane >= 3) & (lane < 6), 1.0, 0.0)
                for idx, c in enumerate(cum_pieces(hh, sl, RC)):
                    aug = jnp.where(lane == idx, c, aug)
                ka_ref[hh, sl, HEAD_DIM:HEAD_DIM + LANES] = aug.astype(MXU_DT)
                vt_ref[hh, :, sl] = v_ref[sl, hcols].T.astype(MXU_DT)

    q0 = pl.multiple_of(i * TQ, TQ)
    lane = lax.broadcasted_iota(jnp.int32, (TQ, LANES), 1)
    qa = []
    for hh in heads:
        qaug = jnp.where(lane < 3, -1.0, 0.0)
        for idx, c in enumerate(cum_pieces(hh, pl.ds(q0, TQ), TQ)):
            qaug = jnp.where(lane == 3 + idx, c, qaug)
        qh = q_ref[:, hh * HEAD_DIM:(hh + 1) * HEAD_DIM] * QSCALE
        qa.append(jnp.concatenate([qh.astype(MXU_DT), qaug.astype(MXU_DT)], axis=1))
    m_ref[...] = jnp.full(m_ref.shape, NEG_INF, F32)
    l_ref[...] = jnp.zeros(l_ref.shape, F32)
    acc_ref[...] = jnp.zeros(acc_ref.shape, F32)

    def scores(t, s_ref):
        k0 = pl.multiple_of(t * TQ, TQ)
        for hh in heads:
            s_ref[hh] = _dot_nt(ka_ref[hh, pl.ds(k0, TQ), :], qa[hh])

    def consume(t, s_ref, diagonal):
        k0 = pl.multiple_of(t * TQ, TQ)
        for hh in heads:
            st = s_ref[hh]
            if diagonal:
                krow = lax.broadcasted_iota(jnp.int32, (TQ, TQ), 0)
                qcol = lax.broadcasted_iota(jnp.int32, (TQ, TQ), 1)
                st = jnp.where(krow <= qcol, st, NEG_INF)
            _online_softmax_step_t(st, vt_ref[hh, :, pl.ds(k0, TQ)],
                                   m_ref.at[hh], l_ref.at[hh], acc_ref.at[hh])

    scores(0, sa_ref)

    def pair(jj, carry):
        scores(2 * jj + 1, sb_ref)
        consume(2 * jj, sa_ref, False)
        scores(2 * jj + 2, sa_ref)
        consume(2 * jj + 1, sb_ref, False)
        return carry

    lax.fori_loop(0, i // 2, pair, 0)

    @pl.when(i % 2 == 0)
    def _():
        consume(i, sa_ref, True)

    @pl.when(i % 2 == 1)
    def _():
        scores(i, sb_ref)
        consume(i - 1, sa_ref, False)
        consume(i, sb_ref, True)

    for hh in heads:
        o_ref[:, hh * HEAD_DIM:(hh + 1) * HEAD_DIM] = (acc_ref[hh] / l_ref[hh]).T.astype(o_ref.dtype)


def _fox(proj, cum, B, T):
    TQ = 512
    NH = 4
    nq = T // TQ
    hw = NH * HEAD_DIM
    one = pl.Buffered(1)
    return pl.pallas_call(
        functools.partial(_fox_body, T=T, TQ=TQ, NH=NH),
        out_shape=jax.ShapeDtypeStruct((B * T, FOX_W), MXU_DT),
        grid=(B, FOX_HEADS // NH, nq),
        in_specs=[pl.BlockSpec((TQ, hw), lambda b, h, i: (b * nq + i, CB_FQ // NH + h)),
                  pl.BlockSpec((T, hw), lambda b, h, i: (b, CB_FK // NH + h), pipeline_mode=one),
                  pl.BlockSpec((T, hw), lambda b, h, i: (b, CB_FV // NH + h), pipeline_mode=one),
                  pl.BlockSpec((T, LANES), lambda b, h, i: (b, 0), pipeline_mode=one)],
        out_specs=pl.BlockSpec((TQ, hw), lambda b, h, i: (b * nq + i, h)),
        scratch_shapes=[pltpu.VMEM((NH, T, HEAD_DIM + LANES), MXU_DT),
                        pltpu.VMEM((NH, HEAD_DIM, T), MXU_DT),
                        pltpu.VMEM((NH, 1, TQ), F32),
                        pltpu.VMEM((NH, 1, TQ), F32),
                        pltpu.VMEM((NH, HEAD_DIM, TQ), F32),
                        pltpu.VMEM((NH, TQ, TQ), F32),
                        pltpu.VMEM((NH, TQ, TQ), F32)],
        compiler_params=_cparams(3),
        name="fox",
    )(proj, proj, proj, cum)


def _gelu_tanh(x):
    return 0.5 * x * (1.0 + jnp.tanh(np.sqrt(2.0 / np.pi).astype(np.float32)
                                     * (x + 0.044715 * (x * x * x))))


def _compress_one(x_ref, pe_ref, w1_ref, w2_ref, NCP):
    h0 = jnp.zeros((NCP, CMP_HIDDEN), F32)
    h1 = jnp.zeros((NCP, CMP_HIDDEN), F32)
    for l in range(CMP_STRIDE):
        xl = x_ref[pl.ds(l, NCP, stride=CMP_STRIDE), :]
        a0 = (xl + pe_ref[l:l + 1, :]).astype(MXU_DT)
        a1 = (xl + pe_ref[CMP_STRIDE + l:CMP_STRIDE + l + 1, :]).astype(MXU_DT)
        h0 = h0 + _dot(a0, w1_ref[l * HEAD_DIM:(l + 1) * HEAD_DIM, :])
        h1 = h1 + _dot(a1, w1_ref[(CMP_STRIDE + l) * HEAD_DIM:(CMP_STRIDE + l + 1) * HEAD_DIM, :])
    hsum = h0 + pltpu.roll(h1, NCP - 1, 0)
    return _dot(_gelu_tanh(hsum).astype(MXU_DT), w2_ref[...])


def _compress_body(xk_ref, xv_ref, pek_ref, w1k_ref, w2k_ref, pev_ref, w1v_ref, w2v_ref,
                   cc_ref, sc_ref, kc_ref, vc_ref, *, NCP):
    kc = _compress_one(xk_ref, pek_ref, w1k_ref, w2k_ref, NCP)
    kc_ref[0, 0] = _rope(kc, cc_ref[...], sc_ref[...]).astype(kc_ref.dtype)
    vc = _compress_one(xv_ref, pev_ref, w1v_ref, w2v_ref, NCP)
    vc_ref[0, 0] = vc.T.astype(vc_ref.dtype)


def _compress(proj, pek, w1k, w2k, pev, w1v, w2v, cc, sc, B, T):
    NCP = T // CMP_STRIDE
    full = lambda shape: pl.BlockSpec(shape, lambda b, g: (0,) * len(shape))
    k_sds = jax.ShapeDtypeStruct((B, NSA_KV_GROUPS, NCP, HEAD_DIM), MXU_DT)
    v_sds = jax.ShapeDtypeStruct((B, NSA_KV_GROUPS, HEAD_DIM, NCP), MXU_DT)
    k_spec = pl.BlockSpec((1, 1, NCP, HEAD_DIM), lambda b, g: (b, g, 0, 0))
    v_spec = pl.BlockSpec((1, 1, HEAD_DIM, NCP), lambda b, g: (b, g, 0, 0))
    return pl.pallas_call(
        functools.partial(_compress_body, NCP=NCP),
        out_shape=(k_sds, v_sds),
        grid=(B, NSA_KV_GROUPS),
        in_specs=[pl.BlockSpec((T, LANES), lambda b, g: (b, CB_NKC + g)),
                  pl.BlockSpec((T, LANES), lambda b, g: (b, CB_NVC + g)),
                  full((CMP_LEN, HEAD_DIM)), full((CMP_LEN * HEAD_DIM, CMP_HIDDEN)),
                  full((CMP_HIDDEN, HEAD_DIM)),
                  full((CMP_LEN, HEAD_DIM)), full((CMP_LEN * HEAD_DIM, CMP_HIDDEN)),
                  full((CMP_HIDDEN, HEAD_DIM)),
                  full((NCP, LANES)), full((NCP, LANES))],
        out_specs=(k_spec, v_spec),
        compiler_params=_cparams(2),
        name="compress",
    )(proj, proj, pek, w1k, w2k, pev, w1v, w2v, cc, sc)


def _nsa_body(q_ref, sm_ref, cq_ref, sq_ref, kc_ref, vc_ref, ks_ref, vs_ref, kw_ref, vw_ref,
              ct_ref, st_ref, ovt_ref, o_ref,
              ksa_ref, vst_ref, kwb_ref, vwt_ref, smt_ref, m_ref, l_ref, acc_ref, sa_ref, sb_ref,
              part_ref, *, T, TQ, TK, WSP, NB, NCP, NSEL):
    i = pl.program_id(1)
    HG = NSA_GROUP_SIZE
    NG = NSA_KV_GROUPS
    groups = range(NG)
    R = HG * TQ
    RC = 512

    @pl.when(i == 0)
    def _():
        for c0 in range(0, T, RC):
            sl = slice(c0, c0 + RC)
            ct = ct_ref[sl, :]
            st = st_ref[sl, :]
            kblk = (c0 + lax.broadcasted_iota(jnp.int32, (RC, LANES), 0)) >> SLC_SHIFT
            onehot = jnp.where(kblk == lax.broadcasted_iota(jnp.int32, (RC, LANES), 1), 1.0, 0.0)
            for gg in groups:
                gc = slice(gg * HEAD_DIM, (gg + 1) * HEAD_DIM)
                ksa_ref[gg, sl, 0:HEAD_DIM] = _rope(ks_ref[sl, gc], ct, st).astype(MXU_DT)
                ksa_ref[gg, sl, HEAD_DIM:HEAD_DIM + LANES] = onehot.astype(MXU_DT)
                kwb_ref[gg, sl, :] = _rope(kw_ref[sl, gc], ct, st).astype(MXU_DT)
                vst_ref[gg, :, sl] = vs_ref[sl, gc].T.astype(MXU_DT)
                vwt_ref[gg, :, sl] = vw_ref[sl, gc].T.astype(MXU_DT)

    q0 = i * TQ
    cq = cq_ref[...]
    sq = sq_ref[...]
    qpos = q0 + (lax.broadcasted_iota(jnp.int32, (1, R), 1) & (TQ - 1))
    smt_ref[...] = sm_ref[...].T

    def gate_row(gg, c):
        return jnp.concatenate(
            [_sigmoid(smt_ref[SM_GATE + 3 * (gg * HG + hh) + c:SM_GATE + 3 * (gg * HG + hh) + c + 1, :])
             for hh in range(HG)], axis=1)

    w0 = pl.multiple_of(jnp.clip((i + 1) * TQ - WSP, 0, T - WSP), LANES)
    dist = (q0 + lax.broadcasted_iota(jnp.int32, (WSP, TQ), 1)) \
        - (w0 + lax.broadcasted_iota(jnp.int32, (WSP, TQ), 0))
    wbias = jnp.where((dist >= 0) & (dist < WINDOW), 0.0, NEG_INF)
    wbias = jnp.concatenate([wbias] * HG, axis=1)
    ovt = ovt_ref[...]
    jblk = lax.broadcasted_iota(jnp.int32, (NB, TQ), 0)
    cur = (q0 + lax.broadcasted_iota(jnp.int32, (NB, TQ), 1)) >> SLC_SHIFT
    forced = (jblk == 0) | (jblk == cur) | (jblk == cur - 1)
    future = jblk > cur
    SUB = 8
    sub = lax.broadcasted_iota(jnp.int32, (SUB, TQ), 0)

    qa = []
    for gg in groups:
        qb = jnp.concatenate(
            [_rope(q_ref[:, (gg * HG + hh) * HEAD_DIM:(gg * HG + hh + 1) * HEAD_DIM], cq, sq) * QSCALE
             for hh in range(HG)], axis=0).astype(MXU_DT)

        sc = _dot_nt(kc_ref[0, gg], qb)
        nrow = lax.broadcasted_iota(jnp.int32, (NCP, R), 0)
        last_valid = (qpos - (CMP_LEN - 1)) >> CMP_SHIFT
        sc = jnp.where(nrow <= last_valid, sc, NEG_INF)
        pc = jnp.exp2(sc - jnp.max(sc, axis=0, keepdims=True))
        pc = pc * (jnp.where(qpos >= CMP_LEN - 1, 1.0, 0.0) / jnp.sum(pc, axis=0, keepdims=True))
        o_c = _dot(vc_ref[0, gg], pc.astype(MXU_DT))

        sw = _dot_nt(kwb_ref[gg, pl.ds(w0, WSP), :], qb) + wbias
        pw = jnp.exp2(sw - jnp.max(sw, axis=0, keepdims=True))
        o_w = _dot(vwt_ref[gg, :, pl.ds(w0, WSP)], pw.astype(MXU_DT)) / jnp.sum(pw, axis=0, keepdims=True)
        part_ref[gg] = gate_row(gg, 0) * o_c + gate_row(gg, 2) * o_w

        psum = (pc[:, 0:TQ] + pc[:, TQ:2 * TQ]) + (pc[:, 2 * TQ:3 * TQ] + pc[:, 3 * TQ:4 * TQ])
        hi, mid, lo = _split3(psum)
        imp = (_dot(ovt, hi) + _dot(ovt, mid)) + _dot(ovt, lo)
        imp = jnp.where(forced, FORCE_SCORE, jnp.where(future, -1.0, imp))
        blocks = [imp[r * SUB:(r + 1) * SUB, :] for r in range(NB // SUB)]
        ranks = [jnp.zeros((SUB, TQ), F32) for _ in blocks]
        for jp in range(NB):
            gj, sj = divmod(jp, SUB)
            row = blocks[gj][sj:sj + 1, :]
            for r, blk in enumerate(blocks):
                if r > gj:
                    beats = jnp.where(row >= blk, 1.0, 0.0)
                elif r < gj:
                    beats = jnp.where(row > blk, 1.0, 0.0)
                else:
                    beats = jnp.where(sub > sj, jnp.where(row >= blk, 1.0, 0.0),
                                      jnp.where(row > blk, 1.0, 0.0))
                ranks[r] = ranks[r] + beats
        rank = jnp.concatenate(ranks, axis=0)
        sel_t = jnp.where((rank < NSEL) & (imp >= 0.0), 1.0, 0.0)
        if NB < LANES:
            sel_t = jnp.concatenate([sel_t, jnp.zeros((LANES - NB, TQ), F32)], axis=0)
        selneg = jnp.where(sel_t.T > 0.5, 0.0, NEG_INF).astype(MXU_DT)
        qa.append(jnp.concatenate([qb, jnp.concatenate([selneg] * HG, axis=0)], axis=1))

    m_ref[...] = jnp.full(m_ref.shape, NEG_INF, F32)
    l_ref[...] = jnp.zeros(l_ref.shape, F32)
    acc_ref[...] = jnp.zeros(acc_ref.shape, F32)

    def scores(t, s_ref):
        k0 = pl.multiple_of(t * TK, TK)
        for gg in groups:
            s_ref[gg] = _dot_nt(ksa_ref[gg, pl.ds(k0, TK), :], qa[gg])

    def consume(t, s_ref, causal):
        k0 = pl.multiple_of(t * TK, TK)
        for gg in groups:
            if causal:
                own = pl.ds(pl.multiple_of(q0 - k0, TQ), TQ)
                krow = lax.broadcasted_iota(jnp.int32, (TQ, TQ), 0)
                qcol = lax.broadcasted_iota(jnp.int32, (TQ, TQ), 1)
                tri = jnp.where(krow <= qcol, 0.0, NEG_INF)
                s_ref[gg, own, :] = s_ref[gg, own, :] + jnp.concatenate([tri] * HG, axis=1)
            _online_softmax_step_t(s_ref[gg], vst_ref[gg, :, pl.ds(k0, TK)],
                                   m_ref.at[gg], l_ref.at[gg], acc_ref.at[gg])

    last = ((i + 1) * TQ + TK - 1) // TK - 1
    scores(0, sa_ref)

    def pair(jj, carry):
        scores(2 * jj + 1, sb_ref)
        consume(2 * jj, sa_ref, False)
        scores(2 * jj + 2, sa_ref)
        consume(2 * jj + 1, sb_ref, False)
        return carry

    lax.fori_loop(0, last // 2, pair, 0)

    @pl.when(last % 2 == 0)
    def _():
        consume(last, sa_ref, True)

    @pl.when(last % 2 == 1)
    def _():
        scores(last, sb_ref)
        consume(last - 1, sa_ref, False)
        consume(last, sb_ref, True)

    for gg in groups:
        o = part_ref[gg] + (gate_row(gg, 1) / l_ref[gg]) * acc_ref[gg]
        for hh in range(HG):
            oc = (gg * HG + hh) * HEAD_DIM
            o_ref[:, oc:oc + HEAD_DIM] = o[:, hh * TQ:(hh + 1) * TQ].T.astype(o_ref.dtype)


def _nsa(proj, kc, vc, ct, st, ovt, B, T):
    TQ = 128
    TK = 512
    WSP = WINDOW + TQ
    NB = T // SLC_LEN
    NCP = T // CMP_STRIDE
    nq = T // TQ
    HG = NSA_GROUP_SIZE
    NG = NSA_KV_GROUPS
    R = HG * TQ
    one = pl.Buffered(1)
    kv_spec = lambda cb: pl.BlockSpec((T, NG * LANES), lambda b, i: (b, cb // NG), pipeline_mode=one)
    return pl.pallas_call(
        functools.partial(_nsa_body, T=T, TQ=TQ, TK=TK, WSP=WSP, NB=NB, NCP=NCP,
                          NSEL=min(SLC_TOP, NB)),
        out_shape=jax.ShapeDtypeStruct((B * T, NSA_W), MXU_DT),
        grid=(B, nq),
        in_specs=[pl.BlockSpec((TQ, NSA_W), lambda b, i: (b * nq + i, CB_NQ * LANES // NSA_W)),
                  pl.BlockSpec((TQ, LANES), lambda b, i: (b * nq + i, CB_SM)),
                  pl.BlockSpec((TQ, LANES), lambda b, i: (i, 0)),
                  pl.BlockSpec((TQ, LANES), lambda b, i: (i, 0)),
                  pl.BlockSpec((1, NG, NCP, HEAD_DIM), lambda b, i: (b, 0, 0, 0)),
                  pl.BlockSpec((1, NG, HEAD_DIM, NCP), lambda b, i: (b, 0, 0, 0)),
                  kv_spec(CB_NKS), kv_spec(CB_NVS), kv_spec(CB_NKW), kv_spec(CB_NVW),
                  pl.BlockSpec((T, LANES), lambda b, i: (0, 0), pipeline_mode=one),
                  pl.BlockSpec((T, LANES), lambda b, i: (0, 0), pipeline_mode=one),
                  pl.BlockSpec((NB, NCP), lambda b, i: (0, 0))],
        out_specs=pl.BlockSpec((TQ, NSA_W), lambda b, i: (b * nq + i, 0)),
        scratch_shapes=[pltpu.VMEM((NG, T, HEAD_DIM + LANES), MXU_DT),
                        pltpu.VMEM((NG, HEAD_DIM, T), MXU_DT),
                        pltpu.VMEM((NG, T, LANES), MXU_DT),
                        pltpu.VMEM((NG, HEAD_DIM, T), MXU_DT),
                        pltpu.VMEM((LANES, TQ), F32),
                        pltpu.VMEM((NG, 1, R), F32),
                        pltpu.VMEM((NG, 1, R), F32),
                        pltpu.VMEM((NG, HEAD_DIM, R), F32),
                        pltpu.VMEM((NG, TK, R), F32),
                        pltpu.VMEM((NG, TK, R), F32),
                        pltpu.VMEM((NG, HEAD_DIM, R), F32)],
        compiler_params=_cparams(2),
        name="nsa",
    )(proj, proj, ct, st, kc, vc, proj, proj, proj, proj, ct, st, ovt)


def _merge_body(of_ref, on_ref, wf_ref, wn_ref, gf_ref, gn_ref, o_ref, *, CN):
    of = of_ref[...]
    on = on_ref[...]
    for c0 in range(0, D_MODEL, CN):
        cols = slice(c0, c0 + CN)
        yf = _dot(of, wf_ref[:, cols])
        yn = _dot(on, wn_ref[:, cols])
        o_ref[:, cols] = (_sigmoid(gf_ref[:, cols]) * yf + _sigmoid(gn_ref[:, cols]) * yn).astype(o_ref.dtype)


def _merge(o_fox, o_nsa, wf, wn, layer, proj):
    m = o_fox.shape[0]
    tm = min(512, m)
    d = D_MODEL
    one = pl.Buffered(1)
    return pl.pallas_call(
        functools.partial(_merge_body, CN=2 * MXU_N),
        out_shape=jax.ShapeDtypeStruct((m, d), MXU_DT),
        grid=(m // tm,),
        in_specs=[pl.BlockSpec((tm, FOX_W), lambda i: (i, 0)),
                  pl.BlockSpec((tm, NSA_W), lambda i: (i, 0)),
                  pl.BlockSpec((None, FOX_W, d), lambda i: (layer, 0, 0), pipeline_mode=one),
                  pl.BlockSpec((None, NSA_W, d), lambda i: (layer, 0, 0), pipeline_mode=one),
                  pl.BlockSpec((tm, d), lambda i: (i, CB_GF * LANES // d)),
                  pl.BlockSpec((tm, d), lambda i: (i, CB_GN * LANES // d))],
        out_specs=pl.BlockSpec((tm, d), lambda i: (i, 0)),
        compiler_params=_cparams(1),
        name="merge",
    )(o_fox, o_nsa, wf, wn, proj, proj)


def _outproj_body(a_ref, w_ref, g_ref, x_ref, o_ref):
    y = _dot(a_ref[...], w_ref[...])
    o_ref[...] = x_ref[...] + _rms(y) * g_ref[...]


def _outproj(mix, w, layer, g, x2):
    m, d = x2.shape
    tm = min(512, m)
    return pl.pallas_call(
        _outproj_body,
        out_shape=jax.ShapeDtypeStruct((m, d), F32),
        grid=(m // tm,),
        in_specs=[pl.BlockSpec((tm, d), lambda i: (i, 0)),
                  pl.BlockSpec((None, d, d), lambda i: (layer, 0, 0)),
                  pl.BlockSpec((1, d), lambda i: (0, 0)),
                  pl.BlockSpec((tm, d), lambda i: (i, 0))],
        out_specs=pl.BlockSpec((tm, d), lambda i: (i, 0)),
        compiler_params=_cparams(1),
        name="outproj",
    )(mix, w, g, x2)


def _ffn_up_body(x_ref, g_ref, wg_ref, wu_ref, o_ref, a_ref):
    @pl.when(pl.program_id(1) == 0)
    def _():
        a_ref[...] = (_rms(x_ref[...]) * g_ref[...]).astype(a_ref.dtype)

    a = a_ref[...]
    hg = _dot(a, wg_ref[...].astype(MXU_DT))
    hu = _dot(a, wu_ref[...].astype(MXU_DT))
    o_ref[...] = (hg * _sigmoid(hg) * hu).astype(o_ref.dtype)


def _ffn_up(x2, g, wg, wu, layer):
    m, d = x2.shape
    f = wg.shape[2]
    tm = min(1024, m)
    tn = 512
    return pl.pallas_call(
        _ffn_up_body,
        out_shape=jax.ShapeDtypeStruct((m, f), MXU_DT),
        grid=(m // tm, f // tn),
        in_specs=[pl.BlockSpec((tm, d), lambda i, j: (i, 0)),
                  pl.BlockSpec((1, d), lambda i, j: (0, 0)),
                  pl.BlockSpec((None, d, tn), lambda i, j: (layer, 0, j)),
                  pl.BlockSpec((None, d, tn), lambda i, j: (layer, 0, j))],
        out_specs=pl.BlockSpec((tm, tn), lambda i, j: (i, j)),
        scratch_shapes=[pltpu.VMEM((tm, d), MXU_DT)],
        compiler_params=_cparams(2),
        name="ffn_up",
    )(x2, g, wg, wu)


def _ffn_down_body(h_ref, w_ref, g_ref, x_ref, o_ref):
    y = _dot(h_ref[...], w_ref[...])
    o_ref[...] = x_ref[...] + _rms(y) * g_ref[...]


def _ffn_down(h, w, layer, g, x2):
    m, d = x2.shape
    f = h.shape[1]
    tm = min(256, m)
    return pl.pallas_call(
        _ffn_down_body,
        out_shape=jax.ShapeDtypeStruct((m, d), F32),
        grid=(m // tm,),
        in_specs=[pl.BlockSpec((tm, f), lambda i: (i, 0)),
                  pl.BlockSpec((None, f, d), lambda i: (layer, 0, 0), pipeline_mode=pl.Buffered(1)),
                  pl.BlockSpec((1, d), lambda i: (0, 0)),
                  pl.BlockSpec((tm, d), lambda i: (i, 0))],
        out_specs=pl.BlockSpec((tm, d), lambda i: (i, 0)),
        compiler_params=_cparams(1),
        name="ffn_down",
    )(h, w, g, x2)


def _pack_body(*refs, CW):
    w_refs, o_ref = refs[:-1], refs[-1]
    offs = [int(v) for v in np.cumsum([0] + IN_SIZES)]

    def cols(a, b):
        a0 = (a // LANES) * LANES
        b1 = -(-b // LANES) * LANES
        parts = []
        for c, ref in enumerate(w_refs):
            lo, hi = max(a0, c * CW), min(b1, (c + 1) * CW)
            if lo < hi:
                parts.append(ref[:, lo - c * CW:hi - c * CW])
        wide = parts[0] if len(parts) == 1 else jnp.concatenate(parts, axis=1)
        return wide[:, a - a0:b - a0]

    rows = o_ref.shape[0]
    gates = cols(offs[12], offs[14])
    fox_qkv = cols(offs[0], offs[3])
    nsa_qkv = cols(offs[4], offs[11])
    small = jnp.concatenate(
        [cols(offs[3], offs[4]), cols(offs[11], offs[12]),
         jnp.zeros((rows, NP_COLS - CB_SM * LANES - FOX_HEADS - 3 * NSA_HEADS), o_ref.dtype)], axis=1)
    o_ref[:, CB_GF * LANES:CB_FQ * LANES] = gates.astype(o_ref.dtype)
    o_ref[:, CB_FQ * LANES:CB_NQ * LANES] = fox_qkv.astype(o_ref.dtype)
    o_ref[:, CB_NQ * LANES:CB_SM * LANES] = nsa_qkv.astype(o_ref.dtype)
    o_ref[:, CB_SM * LANES:NP_COLS] = small.astype(o_ref.dtype)


def _pack_w_in(w_all, layer):
    _, d, n_pad = w_all.shape
    tr = 256
    cw = 11 * LANES
    nwin = n_pad // cw
    assert nwin * cw == n_pad
    window = lambda c: pl.BlockSpec((None, tr, cw), lambda r: (layer, r, c))
    return pl.pallas_call(
        functools.partial(_pack_body, CW=cw),
        out_shape=jax.ShapeDtypeStruct((d, NP_COLS), MXU_DT),
        grid=(d // tr,),
        in_specs=[window(c) for c in range(nwin)],
        out_specs=pl.BlockSpec((tr, NP_COLS), lambda r: (r, 0)),
        compiler_params=_cparams(1),
        name="pack_w_in",
    )(*([w_all] * nwin))


def _rope_tables(pos):
    inv_freq = jnp.power(ROPE_THETA, -jnp.arange(ROPE_HALF, dtype=F32) * (2.0 / ROPE_DIM))
    ang = pos[:, None] * inv_freq[None, :]
    cos, sin = jnp.cos(ang), jnp.sin(ang)
    n = pos.shape[0]
    pad = HEAD_DIM - ROPE_DIM
    c = jnp.concatenate([cos, cos, jnp.ones((n, pad), F32)], axis=1)
    s = jnp.concatenate([-sin, sin, jnp.zeros((n, pad), F32)], axis=1)
    return c, s


def _overlap_t(T):
    nb, ncp = T // SLC_LEN, T // CMP_STRIDE
    sc = np.arange(ncp) * CMP_STRIDE
    ss = np.arange(nb) * SLC_LEN
    ov = np.minimum(sc[None, :] + CMP_LEN, ss[:, None] + SLC_LEN) - np.maximum(sc[None, :], ss[:, None])
    ov = np.clip(ov, 0, None) / CMP_LEN
    ov[:, ncp - 1] = 0.0
    return jnp.asarray(ov, dtype=MXU_DT)


def _layer(x2, B, T, layer, n_mix_pre, n_mix_post, n_ffn_pre, n_ffn_post, w_in, f_bias,
           ck_pe, ck_w1, ck_w2, cv_pe, cv_w1, cv_w2, stacked, tables):
    ct, st, cc, sc, ovt = tables
    w_up_fox, w_up_nsa, w_out, w_gate, w_up, w_down = stacked
    row = lambda v: v.reshape(1, -1).astype(F32)
    bf = lambda w: w.astype(MXU_DT)
    proj = _inproj(x2, row(n_mix_pre), _pack_w_in(w_in, layer))
    bias_row = jnp.pad(f_bias.astype(F32), (0, LANES - FOX_HEADS)).reshape(1, LANES)
    cum = _foxcum(proj, bias_row, B, T)
    o_fox = _fox(proj, cum, B, T)
    kc, vc = _compress(proj, ck_pe, bf(ck_w1), bf(ck_w2), cv_pe, bf(cv_w1), bf(cv_w2), cc, sc, B, T)
    o_nsa = _nsa(proj, kc, vc, ct, st, ovt, B, T)
    mix = _merge(o_fox, o_nsa, w_up_fox, w_up_nsa, layer, proj)
    x2 = _outproj(mix, w_out, layer, row(n_mix_post), x2)
    h = _ffn_up(x2, row(n_ffn_pre), w_gate, w_up, layer)
    return _ffn_down(h, w_down, layer, row(n_ffn_post), x2)


@jax.jit
def kernel(x, norm_mix_pre, norm_mix_post, norm_ffn_pre, norm_ffn_post, w_in, fox_forget_bias, cmp_k_pe, cmp_k_w1, cmp_k_w2, cmp_v_pe, cmp_v_w1, cmp_v_w2, w_up_fox, w_up_nsa, w_out, w_ffn_gate, w_ffn_up, w_ffn_down):
    B, T, D = x.shape
    ct, st = _rope_tables(jnp.arange(T, dtype=F32))
    ncp = T // CMP_STRIDE
    cc, sc = _rope_tables((jnp.arange(ncp) * CMP_STRIDE + CMP_LEN - 1).astype(F32))
    tables = (ct, st, cc, sc, _overlap_t(T))
    x2 = x.reshape(B * T, D)
    bf = lambda w: w.astype(MXU_DT)
    stacked = (bf(w_up_fox), bf(w_up_nsa), bf(w_out), w_ffn_gate, w_ffn_up, bf(w_ffn_down))
    w_in = bf(jnp.pad(w_in, ((0, 0), (0, 0), (0, -w_in.shape[2] % LANES))))
    for l in range(w_in.shape[0]):
        x2 = _layer(x2, B, T, l, norm_mix_pre[l], norm_mix_post[l], norm_ffn_pre[l], norm_ffn_post[l],
                    w_in, fox_forget_bias[l],
                    cmp_k_pe[l], cmp_k_w1[l], cmp_k_w2[l], cmp_v_pe[l], cmp_v_w1[l], cmp_v_w2[l],
                    stacked, tables)
    return x2.reshape(B, T, D)
```

```python
import functools

import numpy as np
import jax
import jax.numpy as jnp
from jax import lax
from jax.experimental import pallas as pl
from jax.experimental.pallas import tpu as pltpu

D_MODEL = 2048
HEAD_DIM = 128
FOX_HEADS = 8
NSA_HEADS = 8
NSA_GROUP_SIZE = 4
NSA_KV_GROUPS = 2
FOX_W = FOX_HEADS * HEAD_DIM
NSA_W = NSA_HEADS * HEAD_DIM
NSA_KV_W = NSA_KV_GROUPS * HEAD_DIM
ROPE_DIM = HEAD_DIM // 4
ROPE_HALF = ROPE_DIM // 2
ROPE_THETA = 500000.0
CMP_LEN = 32
CMP_STRIDE = 16
CMP_HIDDEN = 256
SLC_LEN = 64
SLC_SHIFT = SLC_LEN.bit_length() - 1
CMP_SHIFT = CMP_STRIDE.bit_length() - 1
SLC_TOP = 16
WINDOW = 512
FFN_HIDDEN = 5632
EPS = 1e-6
NEG_INF = -1e30
FORCE_SCORE = 1e6
SCALE = HEAD_DIM ** -0.5
LOG2E = float(np.log2(np.e))
QSCALE = SCALE * LOG2E
IN_SIZES = [FOX_W, FOX_W, FOX_W, FOX_HEADS, NSA_W,
            NSA_KV_W, NSA_KV_W, NSA_KV_W, NSA_KV_W, NSA_KV_W, NSA_KV_W,
            NSA_HEADS * 3, D_MODEL, D_MODEL]

LANES = 128
MXU_DT = jnp.bfloat16
VMEM_LIMIT = 56 * 1024 * 1024

CB_GF, CB_GN = 0, 16
CB_FQ, CB_FK, CB_FV, CB_NQ = 32, 40, 48, 56
CB_NKC, CB_NVC, CB_NKS, CB_NVS, CB_NKW, CB_NVW = 64, 66, 68, 70, 72, 74
CB_SM = 76
NP_COLS = 78 * LANES
SM_FF, SM_GATE = 0, FOX_HEADS
MXU_N = 256

F32 = jnp.float32


def _cparams(n_axes):
    return pltpu.CompilerParams(dimension_semantics=("arbitrary",) * n_axes,
                                vmem_limit_bytes=VMEM_LIMIT)


def _rms(x):
    return x * lax.rsqrt(jnp.mean(x * x, axis=-1, keepdims=True) + EPS)


def _dot(a, b):
    return jnp.dot(a, b, preferred_element_type=F32)


def _dot_nt(a, b):
    return lax.dot_general(a, b, (((1,), (1,)), ((), ())), preferred_element_type=F32)


def _split3(x):
    hi = x.astype(MXU_DT)
    r1 = x - hi.astype(F32)
    mid = r1.astype(MXU_DT)
    lo = (r1 - mid.astype(F32)).astype(MXU_DT)
    return hi, mid, lo


def _rope(x, c, s):
    lane = lax.broadcasted_iota(jnp.int32, x.shape, 1)
    partner = jnp.where(lane < ROPE_HALF,
                        pltpu.roll(x, LANES - ROPE_HALF, 1),
                        pltpu.roll(x, ROPE_HALF, 1))
    return x * c + partner * s


def _sigmoid(z):
    return 1.0 / (1.0 + jnp.exp(-z))


def _inproj_body(x_ref, g_ref, w_ref, o_ref, a_ref):
    @pl.when(pl.program_id(1) == 0)
    def _():
        a_ref[...] = (_rms(x_ref[...]) * g_ref[...]).astype(a_ref.dtype)

    o_ref[...] = _dot(a_ref[...], w_ref[...])


def _inproj(x2, g, w):
    m, d = x2.shape
    n = w.shape[1]
    tm = min(512, m)
    tn = 13 * MXU_N
    return pl.pallas_call(
        _inproj_body,
        out_shape=jax.ShapeDtypeStruct((m, n), F32),
        grid=(m // tm, n // tn),
        in_specs=[pl.BlockSpec((tm, d), lambda i, j: (i, 0)),
                  pl.BlockSpec((1, d), lambda i, j: (0, 0)),
                  pl.BlockSpec((d, tn), lambda i, j: (0, j))],
        out_specs=pl.BlockSpec((tm, tn), lambda i, j: (i, j)),
        scratch_shapes=[pltpu.VMEM((tm, d), MXU_DT)],
        compiler_params=_cparams(2),
        name="inproj",
    )(x2, g, w)


def _foxcum_body(s_ref, b_ref, cum_ref, *, T, CH):
    r = lax.broadcasted_iota(jnp.int32, (CH, CH), 0)
    c = lax.broadcasted_iota(jnp.int32, (CH, CH), 1)
    tri = jnp.where(c <= r, 1.0, 0.0).astype(MXU_DT)
    carry = jnp.zeros((1, LANES), F32)
    for ci in range(T // CH):
        z = s_ref[ci * CH:(ci + 1) * CH, :] + b_ref[...]
        lf = jnp.minimum(z, 0.0) - jnp.log1p(jnp.exp(-jnp.abs(z)))
        hi, mid, lo = _split3(lf)
        out = (_dot(tri, hi) + _dot(tri, mid)) + _dot(tri, lo) + carry
        cum_ref[ci * CH:(ci + 1) * CH, :] = out
        carry = out[CH - 1:CH, :]


def _foxcum(proj, bias_row, B, T):
    CH = 256
    return pl.pallas_call(
        functools.partial(_foxcum_body, T=T, CH=CH),
        out_shape=jax.ShapeDtypeStruct((B * T, LANES), F32),
        grid=(B,),
        in_specs=[pl.BlockSpec((T, LANES), lambda b: (b, CB_SM)),
                  pl.BlockSpec((1, LANES), lambda b: (0, 0))],
        out_specs=pl.BlockSpec((T, LANES), lambda b: (b, 0)),
        compiler_params=_cparams(1),
        name="foxcum",
    )(proj, bias_row)


def _online_softmax_step_t(st, vt_t, m_ref, l_ref, acc_ref):
    m_old = m_ref[...]
    m_new = jnp.maximum(m_old, jnp.max(st, axis=0, keepdims=True))
    alpha = jnp.exp2(m_old - m_new)
    p = jnp.exp2(st - m_new)
    l_ref[...] = alpha * l_ref[...] + jnp.sum(p, axis=0, keepdims=True)
    acc_ref[...] = alpha * acc_ref[...] + _dot(vt_t, p.astype(MXU_DT))
    m_ref[...] = m_new


def _fox_body(q_ref, k_ref, v_ref, cum_ref, o_ref,
              ka_ref, vt_ref, m_ref, l_ref, acc_ref, sa_ref, sb_ref, *, T, TQ, NH):
    hp = pl.program_id(1)
    i = pl.program_id(2)
    RC = 512
    heads = range(NH)

    def cum_pieces(hh, rows, n):
        lane = lax.broadcasted_iota(jnp.int32, (n, LANES), 1)
        col = jnp.sum(jnp.where(lane == hp * NH + hh, cum_ref[rows, :], 0.0), axis=1, keepdims=True)
        return [p.astype(F32) for p in _split3(col * LOG2E)]

    @pl.when(i == 0)
    def _():
        for hh in heads:
            hcols = slice(hh * HEAD_DIM, (hh + 1) * HEAD_DIM)
            for c0 in range(0, T, RC):
                sl = slice(c0, c0 + RC)
                ka_ref[hh, sl, 0:HEAD_DIM] = k_ref[sl, hcols].astype(MXU_DT)
                lane = lax.broadcasted_iota(jnp.int32, (RC, LANES), 1)
                aug = jnp.where((lane >= 3) & (lane < 6), 1.0, 0.0)
                for idx, c in enumerate(cum_pieces(hh, sl, RC)):
                    aug = jnp.where(lane == idx, c, aug)
                ka_ref[hh, sl, HEAD_DIM:HEAD_DIM + LANES] = aug.astype(MXU_DT)
                vt_ref[hh, :, sl] = v_ref[sl, hcols].T.astype(MXU_DT)

    q0 = pl.multiple_of(i * TQ, TQ)
    lane = lax.broadcasted_iota(jnp.int32, (TQ, LANES), 1)
    qa = []
    for hh in heads:
        qaug = jnp.where(lane < 3, -1.0, 0.0)
        for idx, c in enumerate(cum_pieces(hh, pl.ds(q0, TQ), TQ)):
            qaug = jnp.where(lane == 3 + idx, c, qaug)
        qh = q_ref[:, hh * HEAD_DIM:(hh + 1) * HEAD_DIM] * QSCALE
        qa.append(jnp.concatenate([qh.astype(MXU_DT), qaug.astype(MXU_DT)], axis=1))
    m_ref[...] = jnp.full(m_ref.shape, NEG_INF, F32)
    l_ref[...] = jnp.zeros(l_ref.shape, F32)
    acc_ref[...] = jnp.zeros(acc_ref.shape, F32)

    def scores(t, s_ref):
        k0 = pl.multiple_of(t * TQ, TQ)
        for hh in heads:
            s_ref[hh] = _dot_nt(ka_ref[hh, pl.ds(k0, TQ), :], qa[hh])

    def consume(t, s_ref, diagonal):
        k0 = pl.multiple_of(t * TQ, TQ)
        for hh in heads:
            st = s_ref[hh]
            if diagonal:
                krow = lax.broadcasted_iota(jnp.int32, (TQ, TQ), 0)
                qcol = lax.broadcasted_iota(jnp.int32, (TQ, TQ), 1)
                st = jnp.where(krow <= qcol, st, NEG_INF)
            _online_softmax_step_t(st, vt_ref[hh, :, pl.ds(k0, TQ)],
                                   m_ref.at[hh], l_ref.at[hh], acc_ref.at[hh])

    scores(0, sa_ref)

    def pair(jj, carry):
        scores(2 * jj + 1, sb_ref)
        consume(2 * jj, sa_ref, False)
        scores(2 * jj + 2, sa_ref)
        consume(2 * jj + 1, sb_ref, False)
        return carry

    lax.fori_loop(0, i // 2, pair, 0)

    @pl.when(i % 2 == 0)
    def _():
        consume(i, sa_ref, True)

    @pl.when(i % 2 == 1)
    def _():
        scores(i, sb_ref)
        consume(i - 1, sa_ref, False)
        consume(i, sb_ref, True)

    for hh in heads:
        o_ref[:, hh * HEAD_DIM:(hh + 1) * HEAD_DIM] = (acc_ref[hh] / l_ref[hh]).T.astype(o_ref.dtype)


def _fox(proj, cum, B, T):
    TQ = 512
    NH = 4
    nq = T // TQ
    hw = NH * HEAD_DIM
    one = pl.Buffered(1)
    return pl.pallas_call(
        functools.partial(_fox_body, T=T, TQ=TQ, NH=NH),
        out_shape=jax.ShapeDtypeStruct((B * T, FOX_W), MXU_DT),
        grid=(B, FOX_HEADS // NH, nq),
        in_specs=[pl.BlockSpec((TQ, hw), lambda b, h, i: (b * nq + i, CB_FQ // NH + h)),
                  pl.BlockSpec((T, hw), lambda b, h, i: (b, CB_FK // NH + h), pipeline_mode=one),
                  pl.BlockSpec((T, hw), lambda b, h, i: (b, CB_FV // NH + h), pipeline_mode=one),
                  pl.BlockSpec((T, LANES), lambda b, h, i: (b, 0), pipeline_mode=one)],
        out_specs=pl.BlockSpec((TQ, hw), lambda b, h, i: (b * nq + i, h)),
        scratch_shapes=[pltpu.VMEM((NH, T, HEAD_DIM + LANES), MXU_DT),
                        pltpu.VMEM((NH, HEAD_DIM, T), MXU_DT),
                        pltpu.VMEM((NH, 1, TQ), F32),
                        pltpu.VMEM((NH, 1, TQ), F32),
                        pltpu.VMEM((NH, HEAD_DIM, TQ), F32),
                        pltpu.VMEM((NH, TQ, TQ), F32),
                        pltpu.VMEM((NH, TQ, TQ), F32)],
        compiler_params=_cparams(3),
        name="fox",
    )(proj, proj, proj, cum)


def _gelu_tanh(x):
    return 0.5 * x * (1.0 + jnp.tanh(np.sqrt(2.0 / np.pi).astype(np.float32)
                                     * (x + 0.044715 * (x * x * x))))


def _compress_one(x_ref, pe_ref, w1_ref, w2_ref, NCP):
    h0 = jnp.zeros((NCP, CMP_HIDDEN), F32)
    h1 = jnp.zeros((NCP, CMP_HIDDEN), F32)
    for l in range(CMP_STRIDE):
        xl = x_ref[pl.ds(l, NCP, stride=CMP_STRIDE), :]
        a0 = (xl + pe_ref[l:l + 1, :]).astype(MXU_DT)
        a1 = (xl + pe_ref[CMP_STRIDE + l:CMP_STRIDE + l + 1, :]).astype(MXU_DT)
        h0 = h0 + _dot(a0, w1_ref[l * HEAD_DIM:(l + 1) * HEAD_DIM, :])
        h1 = h1 + _dot(a1, w1_ref[(CMP_STRIDE + l) * HEAD_DIM:(CMP_STRIDE + l + 1) * HEAD_DIM, :])
    hsum = h0 + pltpu.roll(h1, NCP - 1, 0)
    return _dot(_gelu_tanh(hsum).astype(MXU_DT), w2_ref[...])


def _compress_body(xk_ref, xv_ref, pek_ref, w1k_ref, w2k_ref, pev_ref, w1v_ref, w2v_ref,
                   cc_ref, sc_ref, kc_ref, vc_ref, *, NCP):
    kc = _compress_one(xk_ref, pek_ref, w1k_ref, w2k_ref, NCP)
    kc_ref[0, 0] = _rope(kc, cc_ref[...], sc_ref[...]).astype(kc_ref.dtype)
    vc = _compress_one(xv_ref, pev_ref, w1v_ref, w2v_ref, NCP)
    vc_ref[0, 0] = vc.T.astype(vc_ref.dtype)


def _compress(proj, pek, w1k, w2k, pev, w1v, w2v, cc, sc, B, T):
    NCP = T // CMP_STRIDE
    full = lambda shape: pl.BlockSpec(shape, lambda b, g: (0,) * len(shape))
    k_sds = jax.ShapeDtypeStruct((B, NSA_KV_GROUPS, NCP, HEAD_DIM), MXU_DT)
    v_sds = jax.ShapeDtypeStruct((B, NSA_KV_GROUPS, HEAD_DIM, NCP), MXU_DT)
    k_spec = pl.BlockSpec((1, 1, NCP, HEAD_DIM), lambda b, g: (b, g, 0, 0))
    v_spec = pl.BlockSpec((1, 1, HEAD_DIM, NCP), lambda b, g: (b, g, 0, 0))
    return pl.pallas_call(
        functools.partial(_compress_body, NCP=NCP),
        out_shape=(k_sds, v_sds),
        grid=(B, NSA_KV_GROUPS),
        in_specs=[pl.BlockSpec((T, LANES), lambda b, g: (b, CB_NKC + g)),
                  pl.BlockSpec((T, LANES), lambda b, g: (b, CB_NVC + g)),
                  full((CMP_LEN, HEAD_DIM)), full((CMP_LEN * HEAD_DIM, CMP_HIDDEN)),
                  full((CMP_HIDDEN, HEAD_DIM)),
                  full((CMP_LEN, HEAD_DIM)), full((CMP_LEN * HEAD_DIM, CMP_HIDDEN)),
                  full((CMP_HIDDEN, HEAD_DIM)),
                  full((NCP, LANES)), full((NCP, LANES))],
        out_specs=(k_spec, v_spec),
        compiler_params=_cparams(2),
        name="compress",
    )(proj, proj, pek, w1k, w2k, pev, w1v, w2v, cc, sc)


def _nsa_body(q_ref, sm_ref, cq_ref, sq_ref, kc_ref, vc_ref, ks_ref, vs_ref, kw_ref, vw_ref,
              ct_ref, st_ref, ovt_ref, o_ref,
              ksa_ref, vst_ref, kwb_ref, vwt_ref, smt_ref, m_ref, l_ref, acc_ref, sa_ref, sb_ref,
              part_ref, *, T, TQ, TK, WSP, NB, NCP, NSEL):
    i = pl.program_id(1)
    HG = NSA_GROUP_SIZE
    NG = NSA_KV_GROUPS
    groups = range(NG)
    R = HG * TQ
    RC = 512

    @pl.when(i == 0)
    def _():
        for c0 in range(0, T, RC):
            sl = slice(c0, c0 + RC)
            ct = ct_ref[sl, :]
            st = st_ref[sl, :]
            kblk = (c0 + lax.broadcasted_iota(jnp.int32, (RC, LANES), 0)) >> SLC_SHIFT
            onehot = jnp.where(kblk == lax.broadcasted_iota(jnp.int32, (RC, LANES), 1), 1.0, 0.0)
            for gg in groups:
                gc = slice(gg * HEAD_DIM, (gg + 1) * HEAD_DIM)
                ksa_ref[gg, sl, 0:HEAD_DIM] = _rope(ks_ref[sl, gc], ct, st).astype(MXU_DT)
                ksa_ref[gg, sl, HEAD_DIM:HEAD_DIM + LANES] = onehot.astype(MXU_DT)
                kwb_ref[gg, sl, :] = _rope(kw_ref[sl, gc], ct, st).astype(MXU_DT)
                vst_ref[gg, :, sl] = vs_ref[sl, gc].T.astype(MXU_DT)
                vwt_ref[gg, :, sl] = vw_ref[sl, gc].T.astype(MXU_DT)

    q0 = i * TQ
    cq = cq_ref[...]
    sq = sq_ref[...]
    qpos = q0 + (lax.broadcasted_iota(jnp.int32, (1, R), 1) & (TQ - 1))
    smt_ref[...] = sm_ref[...].T

    def gate_row(gg, c):
        return jnp.concatenate(
            [_sigmoid(smt_ref[SM_GATE + 3 * (gg * HG + hh) + c:SM_GATE + 3 * (gg * HG + hh) + c + 1, :])
             for hh in range(HG)], axis=1)

    w0 = pl.multiple_of(jnp.clip((i + 1) * TQ - WSP, 0, T - WSP), LANES)
    dist = (q0 + lax.broadcasted_iota(jnp.int32, (WSP, TQ), 1)) \
        - (w0 + lax.broadcasted_iota(jnp.int32, (WSP, TQ), 0))
    wbias = jnp.where((dist >= 0) & (dist < WINDOW), 0.0, NEG_INF)
    wbias = jnp.concatenate([wbias] * HG, axis=1)
    ovt = ovt_ref[...]
    jblk = lax.broadcasted_iota(jnp.int32, (NB, TQ), 0)
    cur = (q0 + lax.broadcasted_iota(jnp.int32, (NB, TQ), 1)) >> SLC_SHIFT
    forced = (jblk == 0) | (jblk == cur) | (jblk == cur - 1)
    future = jblk > cur
    SUB = 8
    sub = lax.broadcasted_iota(jnp.int32, (SUB, TQ), 0)

    qa = []
    for gg in groups:
        qb = jnp.concatenate(
            [_rope(q_ref[:, (gg * HG + hh) * HEAD_DIM:(gg * HG + hh + 1) * HEAD_DIM], cq, sq) * QSCALE
             for hh in range(HG)], axis=0).astype(MXU_DT)

        sc = _dot_nt(kc_ref[0, gg], qb)
        nrow = lax.broadcasted_iota(jnp.int32, (NCP, R), 0)
        last_valid = (qpos - (CMP_LEN - 1)) >> CMP_SHIFT
        sc = jnp.where(nrow <= last_valid, sc, NEG_INF)
        pc = jnp.exp2(sc - jnp.max(sc, axis=0, keepdims=True))
        pc = pc * (jnp.where(qpos >= CMP_LEN - 1, 1.0, 0.0) / jnp.sum(pc, axis=0, keepdims=True))
        o_c = _dot(vc_ref[0, gg], pc.astype(MXU_DT))

        sw = _dot_nt(kwb_ref[gg, pl.ds(w0, WSP), :], qb) + wbias
        pw = jnp.exp2(sw - jnp.max(sw, axis=0, keepdims=True))
        o_w = _dot(vwt_ref[gg, :, pl.ds(w0, WSP)], pw.astype(MXU_DT)) / jnp.sum(pw, axis=0, keepdims=True)
        part_ref[gg] = gate_row(gg, 0) * o_c + gate_row(gg, 2) * o_w

        psum = (pc[:, 0:TQ] + pc[:, TQ:2 * TQ]) + (pc[:, 2 * TQ:3 * TQ] + pc[:, 3 * TQ:4 * TQ])
        hi, mid, lo = _split3(psum)
        imp = (_dot(ovt, hi) + _dot(ovt, mid)) + _dot(ovt, lo)
        imp = jnp.where(forced, FORCE_SCORE, jnp.where(future, -1.0, imp))
        blocks = [imp[r * SUB:(r + 1) * SUB, :] for r in range(NB // SUB)]
        ranks = [jnp.zeros((SUB, TQ), F32) for _ in blocks]
        for jp in range(NB):
            gj, sj = divmod(jp, SUB)
            row = blocks[gj][sj:sj + 1, :]
            for r, blk in enumerate(blocks):
                if r > gj:
                    beats = jnp.where(row >= blk, 1.0, 0.0)
                elif r < gj:
                    beats = jnp.where(row > blk, 1.0, 0.0)
                else:
                    beats = jnp.where(sub > sj, jnp.where(row >= blk, 1.0, 0.0),
                                      jnp.where(row > blk, 1.0, 0.0))
                ranks[r] = ranks[r] + beats
        rank = jnp.concatenate(ranks, axis=0)
        sel_t = jnp.where((rank < NSEL) & (imp >= 0.0), 1.0, 0.0)
        if NB < LANES:
            sel_t = jnp.concatenate([sel_t, jnp.zeros((LANES - NB, TQ), F32)], axis=0)
        selneg = jnp.where(sel_t.T > 0.5, 0.0, NEG_INF).astype(MXU_DT)
        qa.append(jnp.concatenate([qb, jnp.concatenate([selneg] * HG, axis=0)], axis=1))

    m_ref[...] = jnp.full(m_ref.shape, NEG_INF, F32)
    l_ref[...] = jnp.zeros(l_ref.shape, F32)
    acc_ref[...] = jnp.zeros(acc_ref.shape, F32)

    def scores(t, s_ref):
        k0 = pl.multiple_of(t * TK, TK)
        for gg in groups:
            s_ref[gg] = _dot_nt(ksa_ref[gg, pl.ds(k0, TK), :], qa[gg])

    def consume(t, s_ref, causal):
        k0 = pl.multiple_of(t * TK, TK)
        for gg in groups:
            if causal:
                own = pl.ds(pl.multiple_of(q0 - k0, TQ), TQ)
                krow = lax.broadcasted_iota(jnp.int32, (TQ, TQ), 0)
                qcol = lax.broadcasted_iota(jnp.int32, (TQ, TQ), 1)
                tri = jnp.where(krow <= qcol, 0.0, NEG_INF)
                s_ref[gg, own, :] = s_ref[gg, own, :] + jnp.concatenate([tri] * HG, axis=1)
            _online_softmax_step_t(s_ref[gg], vst_ref[gg, :, pl.ds(k0, TK)],
                                   m_ref.at[gg], l_ref.at[gg], acc_ref.at[gg])

    last = ((i + 1) * TQ + TK - 1) // TK - 1
    scores(0, sa_ref)

    def pair(jj, carry):
        scores(2 * jj + 1, sb_ref)
        consume(2 * jj, sa_ref, False)
        scores(2 * jj + 2, sa_ref)
        consume(2 * jj + 1, sb_ref, False)
        return carry

    lax.fori_loop(0, last // 2, pair, 0)

    @pl.when(last % 2 == 0)
    def _():
        consume(last, sa_ref, True)

    @pl.when(last % 2 == 1)
    def _():
        scores(last, sb_ref)
        consume(last - 1, sa_ref, False)
        consume(last, sb_ref, True)

    for gg in groups:
        o = part_ref[gg] + (gate_row(gg, 1) / l_ref[gg]) * acc_ref[gg]
        for hh in range(HG):
            oc = (gg * HG + hh) * HEAD_DIM
            o_ref[:, oc:oc + HEAD_DIM] = o[:, hh * TQ:(hh + 1) * TQ].T.astype(o_ref.dtype)


def _nsa(proj, kc, vc, ct, st, ovt, B, T):
    TQ = 128
    TK = 512
    WSP = WINDOW + TQ
    NB = T // SLC_LEN
    NCP = T // CMP_STRIDE
    nq = T // TQ
    HG = NSA_GROUP_SIZE
    NG = NSA_KV_GROUPS
    R = HG * TQ
    one = pl.Buffered(1)
    kv_spec = lambda cb: pl.BlockSpec((T, NG * LANES), lambda b, i: (b, cb // NG), pipeline_mode=one)
    return pl.pallas_call(
        functools.partial(_nsa_body, T=T, TQ=TQ, TK=TK, WSP=WSP, NB=NB, NCP=NCP,
                          NSEL=min(SLC_TOP, NB)),
        out_shape=jax.ShapeDtypeStruct((B * T, NSA_W), MXU_DT),
        grid=(B, nq),
        in_specs=[pl.BlockSpec((TQ, NSA_W), lambda b, i: (b * nq + i, CB_NQ * LANES // NSA_W)),
                  pl.BlockSpec((TQ, LANES), lambda b, i: (b * nq + i, CB_SM)),
                  pl.BlockSpec((TQ, LANES), lambda b, i: (i, 0)),
                  pl.BlockSpec((TQ, LANES), lambda b, i: (i, 0)),
                  pl.BlockSpec((1, NG, NCP, HEAD_DIM), lambda b, i: (b, 0, 0, 0)),
                  pl.BlockSpec((1, NG, HEAD_DIM, NCP), lambda b, i: (b, 0, 0, 0)),
                  kv_spec(CB_NKS), kv_spec(CB_NVS), kv_spec(CB_NKW), kv_spec(CB_NVW),
                  pl.BlockSpec((T, LANES), lambda b, i: (0, 0), pipeline_mode=one),
                  pl.BlockSpec((T, LANES), lambda b, i: (0, 0), pipeline_mode=one),
                  pl.BlockSpec((NB, NCP), lambda b, i: (0, 0))],
        out_specs=pl.BlockSpec((TQ, NSA_W), lambda b, i: (b * nq + i, 0)),
        scratch_shapes=[pltpu.VMEM((NG, T, HEAD_DIM + LANES), MXU_DT),
                        pltpu.VMEM((NG, HEAD_DIM, T), MXU_DT),
                        pltpu.VMEM((NG, T, LANES), MXU_DT),
                        pltpu.VMEM((NG, HEAD_DIM, T), MXU_DT),
                        pltpu.VMEM((LANES, TQ), F32),
                        pltpu.VMEM((NG, 1, R), F32),
                        pltpu.VMEM((NG, 1, R), F32),
                        pltpu.VMEM((NG, HEAD_DIM, R), F32),
                        pltpu.VMEM((NG, TK, R), F32),
                        pltpu.VMEM((NG, TK, R), F32),
                        pltpu.VMEM((NG, HEAD_DIM, R), F32)],
        compiler_params=_cparams(2),
        name="nsa",
    )(proj, proj, ct, st, kc, vc, proj, proj, proj, proj, ct, st, ovt)


def _merge_body(of_ref, on_ref, wf_ref, wn_ref, gf_ref, gn_ref, o_ref, *, CN):
    of = of_ref[...]
    on = on_ref[...]
    for c0 in range(0, D_MODEL, CN):
        cols = slice(c0, c0 + CN)
        yf = _dot(of, wf_ref[:, cols])
        yn = _dot(on, wn_ref[:, cols])
        o_ref[:, cols] = (_sigmoid(gf_ref[:, cols]) * yf + _sigmoid(gn_ref[:, cols]) * yn).astype(o_ref.dtype)


def _merge(o_fox, o_nsa, wf, wn, layer, proj):
    m = o_fox.shape[0]
    tm = min(512, m)
    d = D_MODEL
    one = pl.Buffered(1)
    return pl.pallas_call(
        functools.partial(_merge_body, CN=2 * MXU_N),
        out_shape=jax.ShapeDtypeStruct((m, d), MXU_DT),
        grid=(m // tm,),
        in_specs=[pl.BlockSpec((tm, FOX_W), lambda i: (i, 0)),
                  pl.BlockSpec((tm, NSA_W), lambda i: (i, 0)),
                  pl.BlockSpec((None, FOX_W, d), lambda i: (layer, 0, 0), pipeline_mode=one),
                  pl.BlockSpec((None, NSA_W, d), lambda i: (layer, 0, 0), pipeline_mode=one),
                  pl.BlockSpec((tm, d), lambda i: (i, CB_GF * LANES // d)),
                  pl.BlockSpec((tm, d), lambda i: (i, CB_GN * LANES // d))],
        out_specs=pl.BlockSpec((tm, d), lambda i: (i, 0)),
        compiler_params=_cparams(1),
        name="merge",
    )(o_fox, o_nsa, wf, wn, proj, proj)


def _outproj_body(a_ref, w_ref, g_ref, x_ref, o_ref):
    y = _dot(a_ref[...], w_ref[...])
    o_ref[...] = x_ref[...] + _rms(y) * g_ref[...]


def _outproj(mix, w, layer, g, x2):
    m, d = x2.shape
    tm = min(512, m)
    return pl.pallas_call(
        _outproj_body,
        out_shape=jax.ShapeDtypeStruct((m, d), F32),
        grid=(m // tm,),
        in_specs=[pl.BlockSpec((tm, d), lambda i: (i, 0)),
                  pl.BlockSpec((None, d, d), lambda i: (layer, 0, 0)),
                  pl.BlockSpec((1, d), lambda i: (0, 0)),
                  pl.BlockSpec((tm, d), lambda i: (i, 0))],
        out_specs=pl.BlockSpec((tm, d), lambda i: (i, 0)),
        compiler_params=_cparams(1),
        name="outproj",
    )(mix, w, g, x2)


def _ffn_up_body(x_ref, g_ref, wg_ref, wu_ref, o_ref, a_ref):
    @pl.when(pl.program_id(1) == 0)
    def _():
        a_ref[...] = (_rms(x_ref[...]) * g_ref[...]).astype(a_ref.dtype)

    a = a_ref[...]
    hg = _dot(a, wg_ref[...].astype(MXU_DT))
    hu = _dot(a, wu_ref[...].astype(MXU_DT))
    o_ref[...] = (hg * _sigmoid(hg) * hu).astype(o_ref.dtype)


def _ffn_up(x2, g, wg, wu, layer):
    m, d = x2.shape
    f = wg.shape[2]
    tm = min(1024, m)
    tn = 512
    return pl.pallas_call(
        _ffn_up_body,
        out_shape=jax.ShapeDtypeStruct((m, f), MXU_DT),
        grid=(m // tm, f // tn),
        in_specs=[pl.BlockSpec((tm, d), lambda i, j: (i, 0)),
                  pl.BlockSpec((1, d), lambda i, j: (0, 0)),
                  pl.BlockSpec((None, d, tn), lambda i, j: (layer, 0, j)),
                  pl.BlockSpec((None, d, tn), lambda i, j: (layer, 0, j))],
        out_specs=pl.BlockSpec((tm, tn), lambda i, j: (i, j)),
        scratch_shapes=[pltpu.VMEM((tm, d), MXU_DT)],
        compiler_params=_cparams(2),
        name="ffn_up",
    )(x2, g, wg, wu)


def _ffn_down_body(h_ref, w_ref, g_ref, x_ref, o_ref):
    y = _dot(h_ref[...], w_ref[...])
    o_ref[...] = x_ref[...] + _rms(y) * g_ref[...]


def _ffn_down(h, w, layer, g, x2):
    m, d = x2.shape
    f = h.shape[1]
    tm = min(256, m)
    return pl.pallas_call(
        _ffn_down_body,
        out_shape=jax.ShapeDtypeStruct((m, d), F32),
        grid=(m // tm,),
        in_specs=[pl.BlockSpec((tm, f), lambda i: (i, 0)),
                  pl.BlockSpec((None, f, d), lambda i: (layer, 0, 0), pipeline_mode=pl.Buffered(1)),
                  pl.BlockSpec((1, d), lambda i: (0, 0)),
                  pl.BlockSpec((tm, d), lambda i: (i, 0))],
        out_specs=pl.BlockSpec((tm, d), lambda i: (i, 0)),
        compiler_params=_cparams(1),
        name="ffn_down",
    )(h, w, g, x2)


def _pack_body(*refs, CW):
    w_refs, o_ref = refs[:-1], refs[-1]
    offs = [int(v) for v in np.cumsum([0] + IN_SIZES)]

    def cols(a, b):
        a0 = (a // LANES) * LANES
        b1 = -(-b // LANES) * LANES
        parts = []
        for c, ref in enumerate(w_refs):
            lo, hi = max(a0, c * CW), min(b1, (c + 1) * CW)
            if lo < hi:
                parts.append(ref[:, lo - c * CW:hi - c * CW])
        wide = parts[0] if len(parts) == 1 else jnp.concatenate(parts, axis=1)
        return wide[:, a - a0:b - a0]

    rows = o_ref.shape[0]
    gates = cols(offs[12], offs[14])
    fox_qkv = cols(offs[0], offs[3])
    nsa_qkv = cols(offs[4], offs[11])
    small = jnp.concatenate(
        [cols(offs[3], offs[4]), cols(offs[11], offs[12]),
         jnp.zeros((rows, NP_COLS - CB_SM * LANES - FOX_HEADS - 3 * NSA_HEADS), o_ref.dtype)], axis=1)
    o_ref[:, CB_GF * LANES:CB_FQ * LANES] = gates.astype(o_ref.dtype)
    o_ref[:, CB_FQ * LANES:CB_NQ * LANES] = fox_qkv.astype(o_ref.dtype)
    o_ref[:, CB_NQ * LANES:CB_SM * LANES] = nsa_qkv.astype(o_ref.dtype)
    o_ref[:, CB_SM * LANES:NP_COLS] = small.astype(o_ref.dtype)


def _pack_w_in(w_all, layer):
    _, d, n_pad = w_all.shape
    tr = 256
    cw = 11 * LANES
    nwin = n_pad // cw
    assert nwin * cw == n_pad
    window = lambda c: pl.BlockSpec((None, tr, cw), lambda r: (layer, r, c))
    return pl.pallas_call(
        functools.partial(_pack_body, CW=cw),
        out_shape=jax.ShapeDtypeStruct((d, NP_COLS), MXU_DT),
        grid=(d // tr,),
        in_specs=[window(c) for c in range(nwin)],
        out_specs=pl.BlockSpec((tr, NP_COLS), lambda r: (r, 0)),
        compiler_params=_cparams(1),
        name="pack_w_in",
    )(*([w_all] * nwin))


def _rope_tables(pos):
    inv_freq = jnp.power(ROPE_THETA, -jnp.arange(ROPE_HALF, dtype=F32) * (2.0 / ROPE_DIM))
    ang = pos[:, None] * inv_freq[None, :]
    cos, sin = jnp.cos(ang), jnp.sin(ang)
    n = pos.shape[0]
    pad = HEAD_DIM - ROPE_DIM
    c = jnp.concatenate([cos, cos, jnp.ones((n, pad), F32)], axis=1)
    s = jnp.concatenate([-sin, sin, jnp.zeros((n, pad), F32)], axis=1)
    return c, s


def _overlap_t(T):
    nb, ncp = T // SLC_LEN, T // CMP_STRIDE
    sc = np.arange(ncp) * CMP_STRIDE
    ss = np.arange(nb) * SLC_LEN
    ov = np.minimum(sc[None, :] + CMP_LEN, ss[:, None] + SLC_LEN) - np.maximum(sc[None, :], ss[:, None])
    ov = np.clip(ov, 0, None) / CMP_LEN
    ov[:, ncp - 1] = 0.0
    return jnp.asarray(ov, dtype=MXU_DT)


def _layer(x2, B, T, layer, n_mix_pre, n_mix_post, n_ffn_pre, n_ffn_post, w_in, f_bias,
           ck_pe, ck_w1, ck_w2, cv_pe, cv_w1, cv_w2, stacked, tables):
    ct, st, cc, sc, ovt = tables
    w_up_fox, w_up_nsa, w_out, w_gate, w_up, w_down = stacked
    row = lambda v: v.reshape(1, -1).astype(F32)
    bf = lambda w: w.astype(MXU_DT)
    proj = _inproj(x2, row(n_mix_pre), _pack_w_in(w_in, layer))
    bias_row = jnp.pad(f_bias.astype(F32), (0, LANES - FOX_HEADS)).reshape(1, LANES)
    cum = _foxcum(proj, bias_row, B, T)
    o_fox = _fox(proj, cum, B, T)
    kc, vc = _compress(proj, ck_pe, bf(ck_w1), bf(ck_w2), cv_pe, bf(cv_w1), bf(cv_w2), cc, sc, B, T)
    o_nsa = _nsa(proj, kc, vc, ct, st, ovt, B, T)
    mix = _merge(o_fox, o_nsa, w_up_fox, w_up_nsa, layer, proj)
    x2 = _outproj(mix, w_out, layer, row(n_mix_post), x2)
    h = _ffn_up(x2, row(n_ffn_pre), w_gate, w_up, layer)
    return _ffn_down(h, w_down, layer, row(n_ffn_post), x2)


@jax.jit
def kernel(x, norm_mix_pre, norm_mix_post, norm_ffn_pre, norm_ffn_post, w_in, fox_forget_bias, cmp_k_pe, cmp_k_w1, cmp_k_w2, cmp_v_pe, cmp_v_w1, cmp_v_w2, w_up_fox, w_up_nsa, w_out, w_ffn_gate, w_ffn_up, w_ffn_down):
    B, T, D = x.shape
    ct, st = _rope_tables(jnp.arange(T, dtype=F32))
    ncp = T // CMP_STRIDE
    cc, sc = _rope_tables((jnp.arange(ncp) * CMP_STRIDE + CMP_LEN - 1).astype(F32))
    tables = (ct, st, cc, sc, _overlap_t(T))
    x2 = x.reshape(B * T, D)
    bf = lambda w: w.astype(MXU_DT)
    stacked = (bf(w_up_fox), bf(w_up_nsa), bf(w_out), w_ffn_gate, w_ffn_up, bf(w_ffn_down))
    w_in = bf(jnp.pad(w_in, ((0, 0), (0, 0), (0, -w_in.shape[2] % LANES))))
    for l in range(w_in.shape[0]):
        x2 = _layer(x2, B, T, l, norm_mix_pre[l], norm_mix_post[l], norm_ffn_pre[l], norm_ffn_post[l],
                    w_in, fox_forget_bias[l],
                    cmp_k_pe[l], cmp_k_w1[l], cmp_k_w2[l], cmp_v_pe[l], cmp_v_w1[l], cmp_v_w2[l],
                    stacked, tables)
    return x2.reshape(B, T, D)
```
